```python
import jax, jax.numpy as jnp
from jax import lax
import numpy as np

D_MODEL = 1024
BATCH = 16
SEQ = 4096
DEPTH = 2

PLE_DIM = 256
HEAD_DIM = 128
GDN_HEADS = D_MODEL // 256
MOBA_HEADS = D_MODEL // 256
FOX_HEADS = D_MODEL // 128
GDN_W = GDN_HEADS * HEAD_DIM
MOBA_W = MOBA_HEADS * HEAD_DIM
FOX_W = FOX_HEADS * HEAD_DIM
MIX_W_AB = GDN_W + MOBA_W
MIX_W_C = FOX_W
CONV_K = 4
GDN_CHUNK = 64
MOBA_BLOCK = 256
MOBA_TOPK = 3
MOBA_QCHUNK = 16
FOX_QBLOCK = 128
IN_AB = 3 * GDN_W + 2 * GDN_HEADS + 3 * MOBA_W + MIX_W_AB
IN_C = 3 * FOX_W + FOX_HEADS + MIX_W_C
N_EVEN = (DEPTH + 1) // 2
N_ODD = DEPTH // 2
RMS_EPS = 1e-6
NEG = -1e30
F32 = jnp.float32

kernel_name = 'hybrid_gdn_moba_fox'


def rmsnorm(x, g):
    xf = x.astype(F32)
    y = xf * lax.rsqrt(jnp.mean(xf * xf, axis=-1, keepdims=True) + RMS_EPS)
    return (y * g.astype(F32)).astype(x.dtype)


def l2norm(t):
    return t * lax.rsqrt(jnp.sum(t * t, axis=-1, keepdims=True) + RMS_EPS)


def to_heads(t, n):
    b, s, _ = t.shape
    return t.reshape(b, s, n, -1).transpose(0, 2, 1, 3)


def from_heads(t):
    b, h, s, d = t.shape
    return t.transpose(0, 2, 1, 3).reshape(b, s, h * d)


def causal_depthwise_conv(u, w):
    c = u.shape[-1]
    return lax.conv_general_dilated(
        u, w[:, None, :].astype(u.dtype), window_strides=(1,),
        padding=[(w.shape[0] - 1, 0)], dimension_numbers=('NWC', 'WIO', 'NWC'),
        feature_group_count=c)


def alibi_slopes(n):
    return 2.0 ** (-8.0 * (jnp.arange(n, dtype=F32) + 1.0) / n)


def gated_delta_rule(q, k, v, beta, g):
    b, h, s, dk = q.shape
    dv = v.shape[-1]
    c = GDN_CHUNK
    n = s // c
    q, k, v = (t.reshape(b, h, n, c, -1) for t in (q, k, v))
    beta = beta.reshape(b, h, n, c)
    G = jnp.cumsum(g.reshape(b, h, n, c), axis=-1)
    incl = jnp.tril(jnp.ones((c, c), bool))
    strict = jnp.tril(jnp.ones((c, c), bool), -1)
    diff = G[..., :, None] - G[..., None, :]
    decay = jnp.where(incl, jnp.exp(jnp.where(incl, diff, 0.0)), 0.0)
    k_beta = k * beta[..., None]
    a_mat = jnp.where(strict, jnp.einsum('bhnid,bhnjd->bhnij', k_beta, k) * decay, 0.0)
    eye = jnp.eye(c, dtype=q.dtype)
    rhs = jnp.concatenate([v * beta[..., None], k_beta * jnp.exp(G)[..., None]], axis=-1)
    sol = lax.linalg.triangular_solve(eye + a_mat, rhs, left_side=True, lower=True,
                                      unit_diagonal=True)
    u, w = sol[..., :dv], sol[..., dv:]
    qk = jnp.where(incl, jnp.einsum('bhnid,bhnjd->bhnij', q, k) * decay, 0.0)
    q_dec = q * jnp.exp(G)[..., None]
    k_dec = k * jnp.exp(G[..., -1:] - G)[..., None]
    g_last = jnp.exp(G[..., -1])

    def step(state, xs):
        q_i, k_i, u_i, w_i, qk_i, gl_i = xs
        v_new = u_i - jnp.einsum('bhcd,bhde->bhce', w_i, state)
        o = jnp.einsum('bhcd,bhde->bhce', q_i, state) + jnp.einsum('bhij,bhje->bhie', qk_i, v_new)
        state = state * gl_i[..., None, None] + jnp.einsum('bhcd,bhce->bhde', k_i, v_new)
        return state, o

    xs = tuple(jnp.moveaxis(t, 2, 0) for t in (q_dec, k_dec, u, w, qk, g_last))
    state0 = jnp.zeros((b, h, dk, dv), q.dtype)
    _, o = lax.scan(step, state0, xs)
    return jnp.moveaxis(o, 0, 2).reshape(b, h, s, dv)


def moba_attention(q, k, v):
    b, h, s, d = q.shape
    L = MOBA_BLOCK
    QC = MOBA_QCHUNK
    nb = -(-s // L)
    s_pad = nb * L
    pad = [(0, 0), (0, 0), (0, s_pad - s), (0, 0)]
    q, k, v = (jnp.pad(t, pad) for t in (q, k, v))
    scale = d ** -0.5
    slopes = alibi_slopes(h)
    k_blk = k.reshape(b, h, nb, L, d)
    v_blk = v.reshape(b, h, nb, L, d)
    k_mean = jnp.mean(k_blk.astype(F32), axis=3)
    q_blk_id = jnp.arange(s_pad) // L
    gate = jnp.einsum('bhsd,bhnd->bhsn', q.astype(F32), k_mean)
    past = jnp.arange(nb)[None, :] < q_blk_id[:, None]
    gate = jnp.where(past, gate, NEG)
    topk = min(MOBA_TOPK, nb)
    _, idx = lax.top_k(gate, topk)
    nq = s_pad // QC
    q_c = q.reshape(b, h, nq, QC, d).transpose(2, 0, 1, 3, 4)
    idx_c = idx.reshape(b, h, nq, QC, topk).transpose(2, 0, 1, 3, 4)
    bi = jnp.arange(b)[:, None, None, None]
    hi = jnp.arange(h)[None, :, None, None]
    offs = jnp.arange(L)

    def chunk(args):
        ci, qq, ii = args
        t = ci * QC + jnp.arange(QC)
        own = (ci * QC) // L
        kg = k_blk[bi, hi, ii]
        vg = v_blk[bi, hi, ii]
        kpos = ii[..., None] * L + offs
        dist_p = (t[:, None, None] - kpos).astype(F32)
        lp = (jnp.einsum('bhqd,bhqkld->bhqkl', qq, kg).astype(F32) * scale
              - slopes[:, None, None, None] * dist_p)
        lp = jnp.where((ii < own)[..., None], lp, NEG).reshape(b, h, QC, topk * L)
        k_own = lax.dynamic_index_in_dim(k_blk, own, axis=2, keepdims=False)
        v_own = lax.dynamic_index_in_dim(v_blk, own, axis=2, keepdims=False)
        dist_o = (t[:, None] - (own * L + offs)[None, :]).astype(F32)
        lo = (jnp.einsum('bhqd,bhld->bhql', qq, k_own).astype(F32) * scale
              - slopes[:, None, None] * dist_o)
        lo = jnp.where(dist_o >= 0, lo, NEG)
        wts = jax.nn.softmax(jnp.concatenate([lp, lo], axis=-1), axis=-1).astype(qq.dtype)
        wp = wts[..., :topk * L].reshape(b, h, QC, topk, L)
        wo = wts[..., topk * L:]
        return (jnp.einsum('bhqkl,bhqkld->bhqd', wp, vg)
                + jnp.einsum('bhql,bhld->bhqd', wo, v_own))

    out = lax.map(chunk, (jnp.arange(nq), q_c, idx_c))
    return out.transpose(1, 2, 0, 3, 4).reshape(b, h, s_pad, d)[:, :, :s]


def forgetting_attention(q, k, v, log_f):
    b, h, s, d = q.shape
    scale = d ** -0.5
    c = jnp.cumsum(log_f, axis=-1)
    outs = []
    for i in range(s // FOX_QBLOCK):
        lo_, hi_ = i * FOX_QBLOCK, (i + 1) * FOX_QBLOCK
        logits = (jnp.einsum('bhqd,bhkd->bhqk', q[:, :, lo_:hi_], k[:, :, :hi_]).astype(F32) * scale
                  + c[:, :, lo_:hi_, None] - c[:, :, None, :hi_])
        causal = jnp.arange(lo_, hi_)[:, None] >= jnp.arange(hi_)[None, :]
        wts = jax.nn.softmax(jnp.where(causal, logits, NEG), axis=-1).astype(v.dtype)
        outs.append(jnp.einsum('bhqk,bhkd->bhqd', wts, v[:, :, :hi_]))
    return jnp.concatenate(outs, axis=2)


def ab_mixer(hn, w_in, conv_w, a_log, dt_bias, gdn_norm_g, w_out):
    proj = hn @ w_in
    cuts = [3 * GDN_W, 3 * GDN_W + GDN_HEADS, 3 * GDN_W + 2 * GDN_HEADS,
            3 * GDN_W + 2 * GDN_HEADS + 3 * MOBA_W]
    qkv_a, a_raw, b_raw, qkv_b, z = jnp.split(proj, cuts, axis=-1)
    qkv_a = jax.nn.silu(causal_depthwise_conv(qkv_a, conv_w))
    qa, ka, va = jnp.split(qkv_a, 3, axis=-1)
    qa = l2norm(to_heads(qa, GDN_HEADS).astype(F32)) * (HEAD_DIM ** -0.5)
    ka = l2norm(to_heads(ka, GDN_HEADS).astype(F32))
    va = to_heads(va, GDN_HEADS).astype(F32)
    beta = jax.nn.sigmoid(b_raw.astype(F32)).transpose(0, 2, 1)
    g = (-jnp.exp(a_log.astype(F32))
         * jax.nn.softplus(a_raw.astype(F32) + dt_bias.astype(F32))).transpose(0, 2, 1)
    oa = rmsnorm(gated_delta_rule(qa, ka, va, beta, g), gdn_norm_g).astype(hn.dtype)
    qb, kb, vb = (to_heads(t, MOBA_HEADS) for t in jnp.split(qkv_b, 3, axis=-1))
    ob = moba_attention(qb, kb, vb)
    y = jnp.concatenate([from_heads(oa), from_heads(ob)], axis=-1) * jax.nn.silu(z)
    return y @ w_out


def fox_mixer(hn, w_in, forget_b, w_out):
    proj = hn @ w_in
    qkv, f_raw, z = jnp.split(proj, [3 * FOX_W, 3 * FOX_W + FOX_HEADS], axis=-1)
    q, k, v = (to_heads(t, FOX_HEADS) for t in jnp.split(qkv, 3, axis=-1))
    log_f = jax.nn.log_sigmoid(f_raw.astype(F32) + forget_b.astype(F32)).transpose(0, 2, 1)
    o = forgetting_attention(q, k, v, log_f)
    return (from_heads(o) * jax.nn.silu(z)) @ w_out


def setup_inputs(seed: int = 0) -> dict:
    key = jax.random.key(seed)
    ks = jax.random.split(key, 18)
    nrm = jax.random.normal
    dt = jnp.exp(jax.random.uniform(ks[6], (N_EVEN, GDN_HEADS), F32, np.log(1e-3), np.log(1e-1)))
    return {
        'x': nrm(ks[0], (BATCH, SEQ, D_MODEL), F32),
        'p': nrm(ks[1], (DEPTH, BATCH, SEQ, PLE_DIM), F32),
        'norm_g': 1.0 + 0.02 * nrm(ks[2], (DEPTH, D_MODEL), F32),
        'w_in_ab': nrm(ks[3], (N_EVEN, D_MODEL, IN_AB), F32) * D_MODEL ** -0.5,
        'conv_w': nrm(ks[4], (N_EVEN, CONV_K, 3 * GDN_W), F32) * CONV_K ** -0.5,
        'a_log': jnp.log(jax.random.uniform(ks[5], (N_EVEN, GDN_HEADS), F32, 1.0, 16.0)),
        'dt_bias': dt + jnp.log(-jnp.expm1(-dt)),
        'gdn_norm_g': 1.0 + 0.02 * nrm(ks[7], (N_EVEN, HEAD_DIM), F32),
        'w_out_ab': nrm(ks[8], (N_EVEN, MIX_W_AB, D_MODEL), F32) * MIX_W_AB ** -0.5,
        'w_in_c': nrm(ks[9], (N_ODD, D_MODEL, IN_C), F32) * D_MODEL ** -0.5,
        'forget_b': jax.random.uniform(ks[10], (N_ODD, FOX_HEADS), F32, 1.0, 5.0),
        'w_out_c': nrm(ks[11], (N_ODD, MIX_W_C, D_MODEL), F32) * MIX_W_C ** -0.5,
        'ple_norm_g': 1.0 + 0.02 * nrm(ks[12], (DEPTH, D_MODEL), F32),
        'w_ple_gate': nrm(ks[13], (DEPTH, D_MODEL, D_MODEL), F32) * D_MODEL ** -0.5,
        'w_ple_proj': nrm(ks[14], (DEPTH, PLE_DIM, D_MODEL), F32) * PLE_DIM ** -0.5,
        'final_g': 1.0 + 0.02 * nrm(ks[15], (D_MODEL,), F32),
    }


def reference(x, p, norm_g, w_in_ab, conv_w, a_log, dt_bias, gdn_norm_g, w_out_ab,
              w_in_c, forget_b, w_out_c, ple_norm_g, w_ple_gate, w_ple_proj, final_g):
    for i in range(DEPTH):
        j = i // 2
        hn = rmsnorm(x, norm_g[i])
        if i % 2 == 0:
            x = x + ab_mixer(hn, w_in_ab[j], conv_w[j], a_log[j], dt_bias[j],
                             gdn_norm_g[j], w_out_ab[j])
        else:
            x = x + fox_mixer(hn, w_in_c[j], forget_b[j], w_out_c[j])
        gate = jax.nn.sigmoid(rmsnorm(x, ple_norm_g[i]) @ w_ple_gate[i])
        x = x + gate * (p[i] @ w_ple_proj[i])
    return rmsnorm(x, final_g)
```

```python
import functools

import jax
import jax.numpy as jnp
from jax import lax
from jax.experimental import pallas as pl
from jax.experimental.pallas import tpu as pltpu

F32 = jnp.float32
BF16 = jnp.bfloat16

HEAD_DIM = 128
PLE_DIM = 256
CONV_K = 4
GDN_CHUNK = 64
MOBA_BLOCK = 256
MOBA_TOPK = 3
RMS_EPS = 1e-6
NEG = -1e30
LANES = 128
SUBLANES = 8
SMALL_ROWS = 16
VMEM_LIMIT = 56 * 1024 * 1024

ROW_TILE = 256
PROJ_CHUNK = 512
GDN_TILE = 256
ATT_TILE = 256


def _sigmoid(x):
    return 1.0 / (1.0 + jnp.exp(-x))


def _silu(x):
    return x * _sigmoid(x)


def _softplus(x):
    return jnp.maximum(x, 0.0) + jnp.log1p(jnp.exp(-jnp.abs(x)))


def _log_sigmoid(x):
    return jnp.minimum(x, 0.0) - jnp.log1p(jnp.exp(-jnp.abs(x)))


def _rms(xf, g):
    return xf * lax.rsqrt(jnp.mean(xf * xf, axis=-1, keepdims=True) + RMS_EPS) * g


def _dot(a, b):
    return jnp.dot(a, b, preferred_element_type=F32)


def _dot_nt(a, b):
    return lax.dot_general(a, b, (((1,), (1,)), ((), ())), preferred_element_type=F32)


def _dot_tn(a, b):
    return lax.dot_general(a, b, (((0,), (0,)), ((), ())), preferred_element_type=F32)


def _seg_cumsum(v, seg, axis):
    pos = lax.broadcasted_iota(jnp.int32, v.shape, axis) & (seg - 1)
    s = 1
    while s < seg:
        v = v + jnp.where(pos >= s, pltpu.roll(v, s, axis=axis), 0.0)
        s *= 2
    return v


def _mix_ple(o_parts, z, x, p, w_out_ref, gple, w_gate_ref, w_proj_ref):
    gz = _silu(z.astype(F32))
    acc = x
    off = 0
    for o in o_parts:
        w = o.shape[1]
        y = (o.astype(F32) * gz[:, off:off + w]).astype(BF16)
        acc = acc + _dot(y, w_out_ref[off:off + w, :])
        off += w
    hn = _rms(acc, gple).astype(BF16)
    gate = _sigmoid(_dot(hn, w_gate_ref[...]))
    pp = _dot(p.astype(BF16), w_proj_ref[...])
    return acc + gate * pp


def _project(hn, w_ref, out_refs):
    off = 0
    for o_ref in out_refs:
        n = o_ref.shape[1]
        for c0 in range(0, n, PROJ_CHUNK):
            c1 = min(n, c0 + PROJ_CHUNK)
            o_ref[:, c0:c1] = _dot(hn, w_ref[:, off + c0:off + c1]).astype(o_ref.dtype)
        off += n


def _rows_a_kernel(x_ref, g_ref, w_ref, ws_ref, wst_ref,
                   qkva_ref, qkvb_ref, z_ref, ab_ref, abt_ref):
    hn = _rms(x_ref[...], g_ref[...]).astype(BF16)
    _project(hn, w_ref, (qkva_ref, qkvb_ref, z_ref))
    ab_ref[...] = _dot(hn, ws_ref[...])
    abt_ref[...] = _dot_nt(wst_ref[...], hn)[:SUBLANES]


def _rows_d_kernel(oa_ref, ob_ref, z_ref, x_ref, p_ref, wout_ref, gple_ref, wgate_ref, wproj_ref,
                   g_ref, w_ref, wst_ref, fb_ref,
                   x2_ref, qkv_ref, z1_ref, c_ref, carry_ref, *, tiles_per_seq):
    x2 = _mix_ple((oa_ref[...], ob_ref[...]), z_ref[...], x_ref[...], p_ref[...],
                  wout_ref, gple_ref[...], wgate_ref, wproj_ref)
    x2_ref[...] = x2
    hn = _rms(x2, g_ref[...]).astype(BF16)
    _project(hn, w_ref, (qkv_ref, z1_ref))

    @pl.when(pl.program_id(0) % tiles_per_seq == 0)
    def _():
        carry_ref[...] = jnp.zeros_like(carry_ref)

    f_raw = _dot_nt(wst_ref[...], hn)[:SUBLANES]
    log_f = _log_sigmoid(f_raw + fb_ref[:, 0:1])
    tm = log_f.shape[1]
    c = _seg_cumsum(log_f, tm, 1) + carry_ref[:, 0:1]
    c_ref[...] = c
    carry_ref[...] = jnp.broadcast_to(c[:, tm - 1:tm], carry_ref.shape)


def _rows_f_kernel(o_ref, z_ref, x_ref, p_ref, wout_ref, gple_ref, wgate_ref, wproj_ref,
                   g_ref, out_ref):
    x2 = _mix_ple((o_ref[...],), z_ref[...], x_ref[...], p_ref[...],
                  wout_ref, gple_ref[...], wgate_ref, wproj_ref)
    out_ref[...] = _rms(x2, g_ref[...])


def _row_spec(tm, n):
    return pl.BlockSpec((tm, n), lambda i: (i, 0))


def _const_spec(shape):
    return pl.BlockSpec(shape, lambda i: (0,) * len(shape), pipeline_mode=pl.Buffered(1))


def _row_params(semantics):
    return pltpu.CompilerParams(dimension_semantics=(semantics,), vmem_limit_bytes=VMEM_LIMIT)


def _inv_unit_lower(a):
    c = a.shape[0]
    eye = (lax.broadcasted_iota(jnp.int32, (c, c), 0)
           == lax.broadcasted_iota(jnp.int32, (c, c), 1)).astype(F32)
    t = eye - a
    p = a
    k = 2
    while k < c:
        pb = p.astype(BF16)
        p = _dot(pb, pb)
        t = t + _dot(t.astype(BF16), p.astype(BF16))
        k *= 2
    return t


def _gdn_kernel(qkv_ref, ab_ref, abt_ref, cw_ref, hp_ref, hpt_ref, gn_ref,
                o_ref, state_ref, tail_ref, *, heads):
    ts = qkv_ref.shape[0]
    width = heads * HEAD_DIM
    ck = GDN_CHUNK

    @pl.when(pl.program_id(1) == 0)
    def _():
        state_ref[...] = jnp.zeros_like(state_ref)
        tail_ref[...] = jnp.zeros_like(tail_ref)

    u = qkv_ref[...].astype(F32)
    prev = tail_ref[...]
    tail_ref[...] = u[ts - SUBLANES:, :]
    cw = cw_ref[...]
    row8 = lax.broadcasted_iota(jnp.int32, prev.shape, 0)
    acc = u * cw[CONV_K - 1:CONV_K, :]
    for s in range(1, CONV_K):
        rolled = pltpu.roll(u, s, axis=0)
        first = jnp.where(row8 < s, pltpu.roll(prev, s, axis=0), rolled[:SUBLANES])
        shifted = jnp.concatenate([first, rolled[SUBLANES:]], axis=0)
        acc = acc + shifted * cw[CONV_K - 1 - s:CONV_K - s, :]
    xc = _silu(acc)

    ab = ab_ref[...]
    hp = hp_ref[...]
    g_col = -jnp.exp(hp[0:1, :]) * _softplus(ab + hp[1:2, :])
    beta_col = _sigmoid(ab)
    gcum_col = _seg_cumsum(g_col, ck, 0)
    hpt = hpt_ref[...]
    g_row = -jnp.exp(hpt[:, 0:1]) * _softplus(abt_ref[...] + hpt[:, 1:2])
    gcum_row = _seg_cumsum(g_row, ck, 1)

    ri = lax.broadcasted_iota(jnp.int32, (ck, ck), 0)
    ci = lax.broadcasted_iota(jnp.int32, (ck, ck), 1)
    incl = ri >= ci
    strict = ri > ci
    gn = gn_ref[...]

    for h in range(heads):
        qh = xc[:, h * HEAD_DIM:(h + 1) * HEAD_DIM]
        kh = xc[:, width + h * HEAD_DIM:width + (h + 1) * HEAD_DIM]
        vh = xc[:, 2 * width + h * HEAD_DIM:2 * width + (h + 1) * HEAD_DIM]
        qn = qh * lax.rsqrt(jnp.sum(qh * qh, axis=-1, keepdims=True) + RMS_EPS) * (HEAD_DIM ** -0.5)
        kn = kh * lax.rsqrt(jnp.sum(kh * kh, axis=-1, keepdims=True) + RMS_EPS)
        state = state_ref[h]
        for c in range(ts // ck):
            r0 = c * ck
            qc, kc, vc = qn[r0:r0 + ck], kn[r0:r0 + ck], vh[r0:r0 + ck]
            bcol = beta_col[r0:r0 + ck, heads + h:heads + h + 1]
            gc = gcum_col[r0:r0 + ck, h:h + 1]
            gr = gcum_row[h:h + 1, r0:r0 + ck]
            decay = jnp.where(incl, jnp.exp(jnp.where(incl, gc - gr, 0.0)), 0.0)
            kb = kc * bcol
            kcb = kc.astype(BF16)
            a_mat = jnp.where(strict, _dot_nt(kb.astype(BF16), kcb) * decay, 0.0)
            t_inv = _inv_unit_lower(a_mat)
            exg = jnp.exp(gc)
            rhs = jnp.concatenate([vc * bcol, kb * exg], axis=1).astype(BF16)
            sol = _dot(t_inv.astype(BF16), rhs)
            u_c, w_c = sol[:, :HEAD_DIM], sol[:, HEAD_DIM:]
            qk = jnp.where(incl, _dot_nt(qc.astype(BF16), kcb) * decay, 0.0)
            g_last = gc[ck - 1:ck, :]
            q_dec = qc * exg
            k_dec = kc * jnp.exp(g_last - gc)
            wq = jnp.concatenate([w_c, q_dec], axis=0).astype(BF16)
            ws = _dot(wq, state.astype(BF16))
            v_new = u_c - ws[:ck]
            vnb = v_new.astype(BF16)
            o = ws[ck:] + _dot(qk.astype(BF16), vnb)
            state = state * jnp.exp(g_last) + _dot_tn(k_dec.astype(BF16), vnb)
            o_ref[r0:r0 + ck, h * HEAD_DIM:(h + 1) * HEAD_DIM] = _rms(o, gn).astype(o_ref.dtype)
        state_ref[h] = state


def _moba_kernel(slopes_ref, q_ref, k_ref, v_ref, o_ref, kmean_ref, *, nb):
    blk = MOBA_BLOCK
    h = pl.program_id(1)
    qi = pl.program_id(2)
    slope = slopes_ref[h]
    scale = HEAD_DIM ** -0.5

    @pl.when(qi == 0)
    def _():
        kmean_ref[...] = jnp.zeros_like(kmean_ref)
        for j in range(nb):
            kmean_ref[j:j + 1, :] = jnp.mean(k_ref[j * blk:(j + 1) * blk, :].astype(F32),
                                             axis=0, keepdims=True)

    q = q_ref[...]
    gate = lax.dot_general(q.astype(F32), kmean_ref[...], (((1,), (1,)), ((), ())),
                           precision=lax.Precision.HIGHEST, preferred_element_type=F32)
    col = lax.broadcasted_iota(jnp.int32, gate.shape, 1)
    past = col < qi
    gm = jnp.where(past, gate, NEG)
    rank = jnp.zeros(gate.shape, jnp.int32)
    for j in range(nb):
        cj = gm[:, j:j + 1]
        beats = (cj > gm) | ((cj == gm) & (col > j))
        rank = rank + beats.astype(jnp.int32)
    sel_bias = jnp.where((rank < MOBA_TOPK) & past, 0.0, NEG)

    r = lax.broadcasted_iota(jnp.int32, (blk, blk), 0)
    c = lax.broadcasted_iota(jnp.int32, (blk, blk), 1)
    alibi = (-slope) * (r - c).astype(F32)

    row0 = pl.multiple_of(qi * blk, blk)
    s = _dot_nt(q, k_ref[pl.ds(row0, blk), :]) * scale + alibi
    s = jnp.where(r >= c, s, NEG)
    m = jnp.max(s, axis=1, keepdims=True)
    p = jnp.exp(s - m)
    l = jnp.sum(p, axis=1, keepdims=True)
    acc = _dot(p.astype(BF16), v_ref[pl.ds(row0, blk), :])

    def body(j, carry):
        m, l, acc = carry
        k0 = pl.multiple_of(j * blk, blk)
        bias = jnp.sum(jnp.where(col == j, sel_bias, 0.0), axis=1, keepdims=True)
        bias = bias - slope * ((qi - j) * blk).astype(F32)
        s = _dot_nt(q, k_ref[pl.ds(k0, blk), :]) * scale + alibi + bias
        m_new = jnp.maximum(m, jnp.max(s, axis=1, keepdims=True))
        alpha = jnp.exp(m - m_new)
        p = jnp.exp(s - m_new)
        l = alpha * l + jnp.sum(p, axis=1, keepdims=True)
        acc = alpha * acc + _dot(p.astype(BF16), v_ref[pl.ds(k0, blk), :])
        return m_new, l, acc

    m, l, acc = lax.fori_loop(0, qi, body, (m, l, acc))
    o_ref[...] = (acc / l).astype(o_ref.dtype)


def _fox_kernel(q_ref, k_ref, v_ref, c_ref, o_ref):
    t = ATT_TILE
    h = pl.program_id(1)
    qi = pl.program_id(2)
    scale = HEAD_DIM ** -0.5
    q = q_ref[...]
    c_q0 = c_ref[h, pl.ds(qi, 1), :][:, 0:1]

    r = lax.broadcasted_iota(jnp.int32, (t, t), 0)
    c = lax.broadcasted_iota(jnp.int32, (t, t), 1)
    row0 = pl.multiple_of(qi * t, t)
    s = _dot_nt(q, k_ref[pl.ds(row0, t), :]) * scale + (c_q0 - c_ref[h, pl.ds(qi, 1), :])
    s = jnp.where(r >= c, s, NEG)
    m = jnp.max(s, axis=1, keepdims=True)
    p = jnp.exp(s - m)
    l = jnp.sum(p, axis=1, keepdims=True)
    acc = _dot(p.astype(BF16), v_ref[pl.ds(row0, t), :])

    def body(j, carry):
        m, l, acc = carry
        k0 = pl.multiple_of(j * t, t)
        s = _dot_nt(q, k_ref[pl.ds(k0, t), :]) * scale + (c_q0 - c_ref[h, pl.ds(j, 1), :])
        m_new = jnp.maximum(m, jnp.max(s, axis=1, keepdims=True))
        alpha = jnp.exp(m - m_new)
        p = jnp.exp(s - m_new)
        l = alpha * l + jnp.sum(p, axis=1, keepdims=True)
        acc = alpha * acc + _dot(p.astype(BF16), v_ref[pl.ds(k0, t), :])
        return m_new, l, acc

    m, l, acc = lax.fori_loop(0, qi, body, (m, l, acc))
    o_ref[...] = (acc / l).astype(o_ref.dtype)


def _pad_cols(w, n):
    return jnp.pad(w, ((0, 0), (0, n - w.shape[1])))


def kernel(x, p, norm_g, w_in_ab, conv_w, a_log, dt_bias, gdn_norm_g, w_out_ab, w_in_c, forget_b,
           w_out_c, ple_norm_g, w_ple_gate, w_ple_proj, final_g):
    b, s, d = x.shape
    t = b * s
    gdn_heads = a_log.shape[1]
    gdn_w = gdn_heads * HEAD_DIM
    moba_w = (w_out_ab.shape[1] - gdn_w)
    moba_heads = moba_w // HEAD_DIM
    fox_heads = forget_b.shape[1]
    fox_w = fox_heads * HEAD_DIM
    mix_ab = gdn_w + moba_w
    assert s % MOBA_BLOCK == 0 and s % ROW_TILE == 0 and s % GDN_TILE == 0 and s % ATT_TILE == 0
    assert 2 * gdn_heads <= SUBLANES and fox_heads <= SUBLANES
    tm = ROW_TILE
    n_row = t // tm

    xf = x.reshape(t, d)
    pf = p.reshape(p.shape[0], t, PLE_DIM)
    row = lambda v: v.reshape(1, -1).astype(F32)

    w0 = w_in_ab[0]
    c1 = 3 * gdn_w
    c2 = c1 + 2 * gdn_heads
    w0_big = jnp.concatenate([w0[:, :c1], w0[:, c2:]], axis=1).astype(BF16)
    w0_small = w0[:, c1:c2]
    w0_s = _pad_cols(w0_small, LANES).astype(BF16)
    w0_st = jnp.pad(w0_small.T, ((0, SMALL_ROWS - 2 * gdn_heads), (0, 0))).astype(BF16)

    qkv_a, qkv_b, z0, ab, abt = pl.pallas_call(
        _rows_a_kernel,
        grid=(n_row,),
        in_specs=[_row_spec(tm, d), _const_spec((1, d)), _const_spec(w0_big.shape),
                  _const_spec(w0_s.shape), _const_spec(w0_st.shape)],
        out_specs=[_row_spec(tm, c1), _row_spec(tm, 3 * moba_w), _row_spec(tm, mix_ab),
                   _row_spec(tm, LANES), pl.BlockSpec((SUBLANES, tm), lambda i: (0, i))],
        out_shape=[jax.ShapeDtypeStruct((t, c1), BF16), jax.ShapeDtypeStruct((t, 3 * moba_w), BF16),
                   jax.ShapeDtypeStruct((t, mix_ab), BF16), jax.ShapeDtypeStruct((t, LANES), F32),
                   jax.ShapeDtypeStruct((SUBLANES, t), F32)],
        compiler_params=_row_params("parallel"),
        name="rows_a",
    )(xf, row(norm_g[0]), w0_big, w0_s, w0_st)

    ts = GDN_TILE
    n_seq = s // ts
    hp = jnp.zeros((SUBLANES, LANES), F32)
    hp = hp.at[0, :gdn_heads].set(a_log[0]).at[1, :gdn_heads].set(dt_bias[0])
    hpt = jnp.zeros((SUBLANES, LANES), F32)
    hpt = hpt.at[:gdn_heads, 0].set(a_log[0]).at[:gdn_heads, 1].set(dt_bias[0])
    oa = pl.pallas_call(
        functools.partial(_gdn_kernel, heads=gdn_heads),
        grid=(b, n_seq),
        in_specs=[pl.BlockSpec((ts, c1), lambda i, j: (i * n_seq + j, 0)),
                  pl.BlockSpec((ts, LANES), lambda i, j: (i * n_seq + j, 0)),
                  pl.BlockSpec((SUBLANES, ts), lambda i, j: (0, i * n_seq + j)),
                  pl.BlockSpec((CONV_K, c1), lambda i, j: (0, 0)),
                  pl.BlockSpec((SUBLANES, LANES), lambda i, j: (0, 0)),
                  pl.BlockSpec((SUBLANES, LANES), lambda i, j: (0, 0)),
                  pl.BlockSpec((1, HEAD_DIM), lambda i, j: (0, 0))],
        out_specs=pl.BlockSpec((ts, gdn_w), lambda i, j: (i * n_seq + j, 0)),
        out_shape=jax.ShapeDtypeStruct((t, gdn_w), BF16),
        scratch_shapes=[pltpu.VMEM((gdn_heads, HEAD_DIM, HEAD_DIM), F32),
                        pltpu.VMEM((SUBLANES, c1), F32)],
        compiler_params=pltpu.CompilerParams(dimension_semantics=("parallel", "arbitrary"),
                                             vmem_limit_bytes=VMEM_LIMIT),
        name="gdn",
    )(qkv_a, ab, abt, conv_w[0].astype(F32), hp, hpt, row(gdn_norm_g[0]))

    nb = s // MOBA_BLOCK
    slopes = 2.0 ** (-8.0 * (jnp.arange(moba_heads, dtype=F32) + 1.0) / moba_heads)
    qkv_b3 = qkv_b.reshape(b, s, 3 * moba_w)
    ob = pl.pallas_call(
        functools.partial(_moba_kernel, nb=nb),
        grid=(b, moba_heads, nb),
        in_specs=[pl.BlockSpec(memory_space=pltpu.SMEM),
                  pl.BlockSpec((None, MOBA_BLOCK, HEAD_DIM), lambda i, h, j: (i, j, h)),
                  pl.BlockSpec((None, s, HEAD_DIM), lambda i, h, j: (i, 0, moba_heads + h)),
                  pl.BlockSpec((None, s, HEAD_DIM), lambda i, h, j: (i, 0, 2 * moba_heads + h))],
        out_specs=pl.BlockSpec((None, MOBA_BLOCK, HEAD_DIM), lambda i, h, j: (i, j, h)),
        out_shape=jax.ShapeDtypeStruct((b, s, moba_w), BF16),
        scratch_shapes=[pltpu.VMEM((LANES, HEAD_DIM), F32)],
        compiler_params=pltpu.CompilerParams(
            dimension_semantics=("parallel", "parallel", "arbitrary"), vmem_limit_bytes=VMEM_LIMIT),
        name="moba",
    )(slopes, qkv_b3, qkv_b3, qkv_b3)
    ob = ob.reshape(t, moba_w)

    w1 = w_in_c[0]
    e1 = 3 * fox_w
    e2 = e1 + fox_heads
    w1_big = jnp.concatenate([w1[:, :e1], w1[:, e2:]], axis=1).astype(BF16)
    w1_st = w1[:, e1:e2].T
    w1_st = jnp.pad(w1_st, ((0, SMALL_ROWS - fox_heads), (0, 0))).astype(BF16)
    fb = jnp.zeros((SUBLANES, LANES), F32).at[:fox_heads, 0].set(forget_b[0])
    ple_w = lambda i: (w_ple_gate[i].astype(BF16), w_ple_proj[i].astype(BF16))
    wg0, wp0 = ple_w(0)
    wout0 = w_out_ab[0].astype(BF16)
    x2, qkv_c, z1, cfox = pl.pallas_call(
        functools.partial(_rows_d_kernel, tiles_per_seq=s // tm),
        grid=(n_row,),
        in_specs=[_row_spec(tm, gdn_w), _row_spec(tm, moba_w), _row_spec(tm, mix_ab), _row_spec(tm, d),
                  _row_spec(tm, PLE_DIM), _const_spec(wout0.shape), _const_spec((1, d)),
                  _const_spec(wg0.shape), _const_spec(wp0.shape), _const_spec((1, d)),
                  _const_spec(w1_big.shape), _const_spec(w1_st.shape), _const_spec(fb.shape)],
        out_specs=[_row_spec(tm, d), _row_spec(tm, e1), _row_spec(tm, fox_w),
                   pl.BlockSpec((SUBLANES, tm), lambda i: (0, i))],
        out_shape=[jax.ShapeDtypeStruct((t, d), F32), jax.ShapeDtypeStruct((t, e1), BF16),
                   jax.ShapeDtypeStruct((t, fox_w), BF16), jax.ShapeDtypeStruct((SUBLANES, t), F32)],
        scratch_shapes=[pltpu.VMEM((SUBLANES, LANES), F32)],
        compiler_params=_row_params("arbitrary"),
        name="rows_d",
    )(oa, ob, z0, xf, pf[0], wout0, row(ple_norm_g[0]), wg0, wp0, row(norm_g[1]), w1_big, w1_st, fb)

    ta = ATT_TILE
    nq = s // ta
    qkv_c3 = qkv_c.reshape(b, s, e1)
    c_tiles = cfox.reshape(SUBLANES, b * nq, ta)
    oc = pl.pallas_call(
        _fox_kernel,
        grid=(b, fox_heads, nq),
        in_specs=[pl.BlockSpec((None, ta, HEAD_DIM), lambda i, h, j: (i, j, h)),
                  pl.BlockSpec((None, s, HEAD_DIM), lambda i, h, j: (i, 0, fox_heads + h)),
                  pl.BlockSpec((None, s, HEAD_DIM), lambda i, h, j: (i, 0, 2 * fox_heads + h)),
                  pl.BlockSpec((SUBLANES, nq, ta), lambda i, h, j: (0, i, 0))],
        out_specs=pl.BlockSpec((None, ta, HEAD_DIM), lambda i, h, j: (i, j, h)),
        out_shape=jax.ShapeDtypeStruct((b, s, fox_w), BF16),
        compiler_params=pltpu.CompilerParams(
            dimension_semantics=("parallel", "parallel", "arbitrary"), vmem_limit_bytes=VMEM_LIMIT),
        name="fox",
    )(qkv_c3, qkv_c3, qkv_c3, c_tiles)
    oc = oc.reshape(t, fox_w)

    wg1, wp1 = ple_w(1)
    wout1 = w_out_c[0].astype(BF16)
    out = pl.pallas_call(
        _rows_f_kernel,
        grid=(n_row,),
        in_specs=[_row_spec(tm, fox_w), _row_spec(tm, fox_w), _row_spec(tm, d), _row_spec(tm, PLE_DIM),
                  _const_spec(wout1.shape), _const_spec((1, d)), _const_spec(wg1.shape),
                  _const_spec(wp1.shape), _const_spec((1, d))],
        out_specs=_row_spec(tm, d),
        out_shape=jax.ShapeDtypeStruct((t, d), F32),
        compiler_params=_row_params("parallel"),
        name="rows_f",
    )(oc, z1, x2, pf[1], wout1, row(ple_norm_g[1]), wg1, wp1, row(final_g))
    return out.reshape(b, s, d)
```

```python
import functools

import jax
import jax.numpy as jnp
from jax import lax
from jax.experimental import pallas as pl
from jax.experimental.pallas import tpu as pltpu

F32 = jnp.float32
BF16 = jnp.bfloat16

HEAD_DIM = 128
PLE_DIM = 256
CONV_K = 4
GDN_CHUNK = 64
MOBA_BLOCK = 256
MOBA_TOPK = 3
RMS_EPS = 1e-6
NEG = -1e30
LANES = 128
SUBLANES = 8
SMALL_ROWS = 16
VMEM_LIMIT = 56 * 1024 * 1024

LOG2E = 1.4426950408889634
ATT_SCALE = HEAD_DIM ** -0.5 * LOG2E

ROW_TILE = 256
PROJ_CHUNK = 512
GDN_TILE = 256
MOBA_HEADS_PER_STEP = 4
FOX_HEADS_PER_STEP = 2
FOX_TQ = 512
FOX_TK = 512


def _sigmoid(x):
    return 1.0 / (1.0 + jnp.exp(-x))


def _silu(x):
    return x * _sigmoid(x)


def _softplus(x):
    return jnp.maximum(x, 0.0) + jnp.log1p(jnp.exp(-jnp.abs(x)))


def _log_sigmoid(x):
    return jnp.minimum(x, 0.0) - jnp.log1p(jnp.exp(-jnp.abs(x)))


def _rms(xf, g):
    return xf * lax.rsqrt(jnp.mean(xf * xf, axis=-1, keepdims=True) + RMS_EPS) * g


def _dot(a, b):
    return jnp.dot(a, b, preferred_element_type=F32)


def _dot_nt(a, b):
    return lax.dot_general(a, b, (((1,), (1,)), ((), ())), preferred_element_type=F32)


def _seg_cumsum(v, seg, axis):
    pos = lax.broadcasted_iota(jnp.int32, v.shape, axis) & (seg - 1)
    s = 1
    while s < seg:
        v = v + jnp.where(pos >= s, pltpu.roll(v, s, axis=axis), 0.0)
        s *= 2
    return v


def _mix_ple(o_parts, z, x, p, w_out_ref, gple, w_gate_ref, w_proj_ref):
    gz = _silu(z.astype(F32))
    acc = x
    off = 0
    for o in o_parts:
        w = o.shape[1]
        y = (o.astype(F32) * gz[:, off:off + w]).astype(BF16)
        acc = acc + _dot(y, w_out_ref[off:off + w, :])
        off += w
    hn = _rms(acc, gple).astype(BF16)
    gate = _sigmoid(_dot(hn, w_gate_ref[...]))
    pp = _dot(p.astype(BF16), w_proj_ref[...])
    return acc + gate * pp


def _project(hn, w_ref, out_refs):
    off = 0
    for o_ref in out_refs:
        n = o_ref.shape[1]
        for c0 in range(0, n, PROJ_CHUNK):
            c1 = min(n, c0 + PROJ_CHUNK)
            o_ref[:, c0:c1] = _dot(hn, w_ref[:, off + c0:off + c1]).astype(o_ref.dtype)
        off += n


def _rows_a_kernel(x_ref, g_ref, w_ref, ws_ref, wst_ref,
                   qkva_ref, qkvb_ref, z_ref, ab_ref, abt_ref):
    hn = _rms(x_ref[...], g_ref[...]).astype(BF16)
    _project(hn, w_ref, (qkva_ref, qkvb_ref, z_ref))
    ab_ref[...] = _dot(hn, ws_ref[...])
    abt_ref[...] = _dot_nt(wst_ref[...], hn)[:SUBLANES]


def _rows_d_kernel(oa_ref, ob_ref, z_ref, x_ref, p_ref, wout_ref, gple_ref, wgate_ref, wproj_ref,
                   g_ref, w_ref, wst_ref, fb_ref,
                   x2_ref, qkv_ref, z1_ref, c_ref, carry_ref, *, tiles_per_seq):
    x2 = _mix_ple((oa_ref[...], ob_ref[...]), z_ref[...], x_ref[...], p_ref[...],
                  wout_ref, gple_ref[...], wgate_ref, wproj_ref)
    x2_ref[...] = x2
    hn = _rms(x2, g_ref[...]).astype(BF16)
    _project(hn, w_ref, (qkv_ref, z1_ref))

    @pl.when(pl.program_id(0) % tiles_per_seq == 0)
    def _():
        carry_ref[...] = jnp.zeros_like(carry_ref)

    f_raw = _dot_nt(wst_ref[...], hn)[:SUBLANES]
    log_f = _log_sigmoid(f_raw + fb_ref[:, 0:1])
    tm = log_f.shape[1]
    c = _seg_cumsum(log_f, tm, 1) + carry_ref[:, 0:1]
    c_ref[...] = c
    carry_ref[...] = jnp.broadcast_to(c[:, tm - 1:tm], carry_ref.shape)


def _rows_f_kernel(o_ref, z_ref, x_ref, p_ref, wout_ref, gple_ref, wgate_ref, wproj_ref,
                   g_ref, out_ref):
    x2 = _mix_ple((o_ref[...],), z_ref[...], x_ref[...], p_ref[...],
                  wout_ref, gple_ref[...], wgate_ref, wproj_ref)
    out_ref[...] = _rms(x2, g_ref[...])


def _row_spec(tm, n):
    return pl.BlockSpec((tm, n), lambda i: (i, 0))


def _const_spec(shape):
    return pl.BlockSpec(shape, lambda i: (0,) * len(shape), pipeline_mode=pl.Buffered(1))


def _row_params(semantics):
    return pltpu.CompilerParams(dimension_semantics=(semantics,), vmem_limit_bytes=VMEM_LIMIT)


def _bdot(a, b):
    return lax.dot_general(a, b, (((2,), (1,)), ((0,), (0,))), preferred_element_type=F32)


def _bdot_nt(a, b):
    return lax.dot_general(a, b, (((2,), (2,)), ((0,), (0,))), preferred_element_type=F32)


def _bdot_tn(a, b):
    return lax.dot_general(a, b, (((1,), (1,)), ((0,), (0,))), preferred_element_type=F32)


def _inv_unit_lower(a):
    c = a.shape[-1]
    eye = (lax.broadcasted_iota(jnp.int32, (c, c), 0)
           == lax.broadcasted_iota(jnp.int32, (c, c), 1)).astype(F32)
    t = eye - a
    p = a
    k = 2
    while k < c:
        pb = p.astype(BF16)
        p = _bdot(pb, pb)
        t = t + _bdot(t.astype(BF16), p.astype(BF16))
        k *= 2
    return t


def _gdn_kernel(qkv_ref, ab_ref, abt_ref, cw_ref, hp_ref, hpt_ref, gn_ref,
                o_ref, state_ref, tail_ref, *, heads):
    ts = qkv_ref.shape[0]
    width = heads * HEAD_DIM
    ck = GDN_CHUNK
    nc = ts // ck

    @pl.when(pl.program_id(1) == 0)
    def _():
        state_ref[...] = jnp.zeros_like(state_ref)
        tail_ref[...] = jnp.zeros_like(tail_ref)

    u = qkv_ref[...].astype(F32)
    prev = tail_ref[...]
    tail_ref[...] = u[ts - SUBLANES:, :]
    cw = cw_ref[...]
    row8 = lax.broadcasted_iota(jnp.int32, prev.shape, 0)
    acc = u * cw[CONV_K - 1:CONV_K, :]
    for s in range(1, CONV_K):
        rolled = pltpu.roll(u, s, axis=0)
        first = jnp.where(row8 < s, pltpu.roll(prev, s, axis=0), rolled[:SUBLANES])
        shifted = jnp.concatenate([first, rolled[SUBLANES:]], axis=0)
        acc = acc + shifted * cw[CONV_K - 1 - s:CONV_K - s, :]
    xc = _silu(acc)

    ab = ab_ref[...]
    hp = hp_ref[...]
    g_col = -jnp.exp(hp[0:1, :]) * _softplus(ab + hp[1:2, :])
    beta_col = _sigmoid(ab)
    gcum_col = _seg_cumsum(g_col, ck, 0)
    hpt = hpt_ref[...]
    g_row = -jnp.exp(hpt[:, 0:1]) * _softplus(abt_ref[...] + hpt[:, 1:2])
    gcum_row = _seg_cumsum(g_row, ck, 1)

    def chunks(col0):
        return jnp.concatenate(
            [xc[:, col0 + h * HEAD_DIM:col0 + (h + 1) * HEAD_DIM].reshape(nc, ck, HEAD_DIM)
             for h in range(heads)], axis=0)

    def col_chunks(a, lane0):
        return jnp.concatenate(
            [a[:, lane0 + h:lane0 + h + 1].reshape(nc, ck, 1) for h in range(heads)], axis=0)

    q3, k3, v3 = chunks(0), chunks(width), chunks(2 * width)
    q3 = q3 * lax.rsqrt(jnp.sum(q3 * q3, axis=-1, keepdims=True) + RMS_EPS) * (HEAD_DIM ** -0.5)
    k3 = k3 * lax.rsqrt(jnp.sum(k3 * k3, axis=-1, keepdims=True) + RMS_EPS)
    beta3 = col_chunks(beta_col, heads)
    gc3 = col_chunks(gcum_col, 0)
    gr3 = jnp.stack([gcum_row[h:h + 1, c * ck:(c + 1) * ck]
                     for h in range(heads) for c in range(nc)], axis=0)

    ri = lax.broadcasted_iota(jnp.int32, (ck, ck), 0)
    ci = lax.broadcasted_iota(jnp.int32, (ck, ck), 1)
    incl = ri >= ci
    strict = ri > ci
    decay = jnp.where(incl, jnp.exp(jnp.where(incl, gc3 - gr3, 0.0)), 0.0)
    kb3 = k3 * beta3
    k3b = k3.astype(BF16)
    a_mat = jnp.where(strict, _bdot_nt(kb3.astype(BF16), k3b) * decay, 0.0)
    t_inv = _inv_unit_lower(a_mat)
    exg = jnp.exp(gc3)
    rhs = jnp.concatenate([v3 * beta3, kb3 * exg], axis=2).astype(BF16)
    sol = _bdot(t_inv.astype(BF16), rhs)
    u3, w3 = sol[:, :, :HEAD_DIM], sol[:, :, HEAD_DIM:]
    qk3 = jnp.where(incl, _bdot_nt(q3.astype(BF16), k3b) * decay, 0.0).astype(BF16)
    g_last = gc3[:, ck - 1:ck, :]
    wq3 = jnp.concatenate([w3, q3 * exg], axis=1).astype(BF16)
    kdec3 = (k3 * jnp.exp(g_last - gc3)).astype(BF16)
    gl3 = jnp.exp(g_last)

    by_head = lambda a: a.reshape((heads, nc) + a.shape[1:])
    u4, qk4, wq4, kdec4, gl4 = (by_head(a) for a in (u3, qk3, wq3, kdec3, gl3))
    gn = gn_ref[...]
    state = state_ref[...]
    for c in range(nc):
        ws = _bdot(wq4[:, c], state.astype(BF16))
        v_new = (u4[:, c] - ws[:, :ck]).astype(BF16)
        o = ws[:, ck:] + _bdot(qk4[:, c], v_new)
        state = state * gl4[:, c] + _bdot_tn(kdec4[:, c], v_new)
        o = _rms(o, gn).astype(o_ref.dtype)
        for h in range(heads):
            o_ref[c * ck:(c + 1) * ck, h * HEAD_DIM:(h + 1) * HEAD_DIM] = o[h]
    state_ref[...] = state


def _softmax_step(q, k, v, bias, row_shift, mask, carry):
    m, l, acc = carry
    s = _dot_nt(q, k) + bias
    if mask is not None:
        s = jnp.where(mask, s, NEG)
    s_max = jnp.max(s, axis=1, keepdims=True)
    if row_shift is None:
        m_new = jnp.maximum(m, s_max)
        p = jnp.exp2(s - m_new)
    else:
        m_new = jnp.maximum(m, s_max + row_shift)
        p = jnp.exp2(s - (m_new - row_shift))
    alpha = jnp.exp2(m - m_new)
    l = alpha * l + jnp.sum(p, axis=1, keepdims=True)
    acc = alpha * acc + _dot(p.astype(BF16), v)
    return m_new, l, acc


def _softmax_init(tq):
    return (jnp.full((tq, 1), NEG, F32), jnp.zeros((tq, 1), F32), jnp.zeros((tq, HEAD_DIM), F32))


def _head_cols(hh):
    return slice(hh * HEAD_DIM, (hh + 1) * HEAD_DIM)


def _moba_kernel(slopes_ref, q_ref, k_ref, v_ref, o_ref, kmean_ref, *, nb, hp):
    blk = MOBA_BLOCK
    hg = pl.program_id(1)
    qi = pl.program_id(2)
    nbp = -(-nb // SUBLANES) * SUBLANES

    @pl.when(qi == 0)
    def _():
        kmean_ref[...] = jnp.zeros_like(kmean_ref)
        for hh in range(hp):
            for j in range(nb):
                kmean_ref[hh, j:j + 1, :] = jnp.mean(
                    k_ref[j * blk:(j + 1) * blk, _head_cols(hh)].astype(F32), axis=0, keepdims=True)

    r = lax.broadcasted_iota(jnp.int32, (blk, blk), 0)
    c = lax.broadcasted_iota(jnp.int32, (blk, blk), 1)
    rc = (r - c).astype(F32)
    blk_id = lax.broadcasted_iota(jnp.int32, (nbp, blk), 0)
    past = blk_id < qi
    col = lax.broadcasted_iota(jnp.int32, (blk, LANES), 1)

    qs, alibis, shifts = [], [], []
    for hh in range(hp):
        slope2 = slopes_ref[hg * hp + hh] * LOG2E
        q = q_ref[:, _head_cols(hh)]
        gate = lax.dot_general(kmean_ref[hh], q.astype(F32), (((1,), (1,)), ((), ())),
                               precision=lax.Precision.HIGHEST, preferred_element_type=F32)[:nbp]
        gm = jnp.where(past, gate, NEG)
        rank = jnp.zeros(gm.shape, jnp.int32)
        for j in range(nb):
            cj = gm[j:j + 1, :]
            beats = (cj > gm) | ((cj == gm) & (blk_id > j))
            rank = rank + beats.astype(jnp.int32)
        shift_t = (jnp.where((rank < MOBA_TOPK) & past, 0.0, NEG)
                   - slope2 * ((qi - blk_id) * blk).astype(F32))
        shift_t = jnp.concatenate([shift_t, jnp.zeros((LANES - nbp, blk), F32)], axis=0)
        qs.append(q)
        alibis.append((-slope2) * rc)
        shifts.append(shift_t.T)

    row0 = pl.multiple_of(qi * blk, blk)
    carries = tuple(
        _softmax_step(qs[hh], k_ref[pl.ds(row0, blk), _head_cols(hh)],
                      v_ref[pl.ds(row0, blk), _head_cols(hh)], alibis[hh], None, r >= c,
                      _softmax_init(blk))
        for hh in range(hp))

    def body(j, carries):
        k0 = pl.multiple_of(j * blk, blk)
        out = []
        for hh in range(hp):
            shift = jnp.sum(jnp.where(col == j, shifts[hh], 0.0), axis=1, keepdims=True)
            out.append(_softmax_step(qs[hh], k_ref[pl.ds(k0, blk), _head_cols(hh)],
                                     v_ref[pl.ds(k0, blk), _head_cols(hh)], alibis[hh], shift, None,
                                     carries[hh]))
        return tuple(out)

    carries = lax.fori_loop(0, qi, body, carries)
    for hh in range(hp):
        _, l, acc = carries[hh]
        o_ref[:, _head_cols(hh)] = (acc / l).astype(o_ref.dtype)


def _fox_kernel(q_ref, k_ref, v_ref, c_ref, o_ref, *, hp, tk):
    tq = q_ref.shape[0]
    nd = tq // tk
    hg = pl.program_id(1)
    qi = pl.program_id(2)
    n_full = qi * nd
    r = lax.broadcasted_iota(jnp.int32, (tq, tk), 0)
    c = lax.broadcasted_iota(jnp.int32, (tq, tk), 1)
    qs = [q_ref[:, _head_cols(hh)] for hh in range(hp)]
    c_q0 = [c_ref[hg * hp + hh, pl.ds(n_full, 1), :][:, 0:1] for hh in range(hp)]

    def step(j, mask, carries):
        k0 = pl.multiple_of(j * tk, tk)
        out = []
        for hh in range(hp):
            bias = (c_q0[hh] - c_ref[hg * hp + hh, pl.ds(j, 1), :]) * LOG2E
            out.append(_softmax_step(qs[hh], k_ref[pl.ds(k0, tk), _head_cols(hh)],
                                     v_ref[pl.ds(k0, tk), _head_cols(hh)], bias, None, mask,
                                     carries[hh]))
        return tuple(out)

    carries = tuple(_softmax_init(tq) for _ in range(hp))
    for dj in range(nd):
        carries = step(n_full + dj, r >= c + dj * tk, carries)
    carries = lax.fori_loop(0, n_full, lambda j, cr: step(j, None, cr), carries)
    for hh in range(hp):
        _, l, acc = carries[hh]
        o_ref[:, _head_cols(hh)] = (acc / l).astype(o_ref.dtype)


def _pad_cols(w, n):
    return jnp.pad(w, ((0, 0), (0, n - w.shape[1])))


def kernel(x, p, norm_g, w_in_ab, conv_w, a_log, dt_bias, gdn_norm_g, w_out_ab, w_in_c, forget_b,
           w_out_c, ple_norm_g, w_ple_gate, w_ple_proj, final_g):
    b, s, d = x.shape
    t = b * s
    gdn_heads = a_log.shape[1]
    gdn_w = gdn_heads * HEAD_DIM
    moba_w = (w_out_ab.shape[1] - gdn_w)
    moba_heads = moba_w // HEAD_DIM
    fox_heads = forget_b.shape[1]
    fox_w = fox_heads * HEAD_DIM
    mix_ab = gdn_w + moba_w
    assert s % MOBA_BLOCK == 0 and s % ROW_TILE == 0 and s % GDN_TILE == 0 and s % FOX_TQ == 0
    assert 2 * gdn_heads <= SUBLANES and fox_heads <= SUBLANES
    tm = ROW_TILE
    n_row = t // tm

    xf = x.reshape(t, d)
    pf = p.reshape(p.shape[0], t, PLE_DIM)
    row = lambda v: v.reshape(1, -1).astype(F32)

    w0 = w_in_ab[0]
    c1 = 3 * gdn_w
    c2 = c1 + 2 * gdn_heads
    w0_big = jnp.concatenate([w0[:, :c1], w0[:, c2:c2 + moba_w] * ATT_SCALE, w0[:, c2 + moba_w:]],
                             axis=1).astype(BF16)
    w0_small = w0[:, c1:c2]
    w0_s = _pad_cols(w0_small, LANES).astype(BF16)
    w0_st = jnp.pad(w0_small.T, ((0, SMALL_ROWS - 2 * gdn_heads), (0, 0))).astype(BF16)

    qkv_a, qkv_b, z0, ab, abt = pl.pallas_call(
        _rows_a_kernel,
        grid=(n_row,),
        in_specs=[_row_spec(tm, d), _const_spec((1, d)), _const_spec(w0_big.shape),
                  _const_spec(w0_s.shape), _const_spec(w0_st.shape)],
        out_specs=[_row_spec(tm, c1), _row_spec(tm, 3 * moba_w), _row_spec(tm, mix_ab),
                   _row_spec(tm, LANES), pl.BlockSpec((SUBLANES, tm), lambda i: (0, i))],
        out_shape=[jax.ShapeDtypeStruct((t, c1), BF16), jax.ShapeDtypeStruct((t, 3 * moba_w), BF16),
                   jax.ShapeDtypeStruct((t, mix_ab), BF16), jax.ShapeDtypeStruct((t, LANES), F32),
                   jax.ShapeDtypeStruct((SUBLANES, t), F32)],
        compiler_params=_row_params("parallel"),
        name="rows_a",
    )(xf, row(norm_g[0]), w0_big, w0_s, w0_st)

    ts = GDN_TILE
    n_seq = s // ts
    hp = jnp.zeros((SUBLANES, LANES), F32)
    hp = hp.at[0, :gdn_heads].set(a_log[0]).at[1, :gdn_heads].set(dt_bias[0])
    hpt = jnp.zeros((SUBLANES, LANES), F32)
    hpt = hpt.at[:gdn_heads, 0].set(a_log[0]).at[:gdn_heads, 1].set(dt_bias[0])
    oa = pl.pallas_call(
        functools.partial(_gdn_kernel, heads=gdn_heads),
        grid=(b, n_seq),
        in_specs=[pl.BlockSpec((ts, c1), lambda i, j: (i * n_seq + j, 0)),
                  pl.BlockSpec((ts, LANES), lambda i, j: (i * n_seq + j, 0)),
                  pl.BlockSpec((SUBLANES, ts), lambda i, j: (0, i * n_seq + j)),
                  pl.BlockSpec((CONV_K, c1), lambda i, j: (0, 0)),
                  pl.BlockSpec((SUBLANES, LANES), lambda i, j: (0, 0)),
                  pl.BlockSpec((SUBLANES, LANES), lambda i, j: (0, 0)),
                  pl.BlockSpec((1, HEAD_DIM), lambda i, j: (0, 0))],
        out_specs=pl.BlockSpec((ts, gdn_w), lambda i, j: (i * n_seq + j, 0)),
        out_shape=jax.ShapeDtypeStruct((t, gdn_w), BF16),
        scratch_shapes=[pltpu.VMEM((gdn_heads, HEAD_DIM, HEAD_DIM), F32),
                        pltpu.VMEM((SUBLANES, c1), F32)],
        compiler_params=pltpu.CompilerParams(dimension_semantics=("parallel", "arbitrary"),
                                             vmem_limit_bytes=VMEM_LIMIT),
        name="gdn",
    )(qkv_a, ab, abt, conv_w[0].astype(F32), hp, hpt, row(gdn_norm_g[0]))

    nb = s // MOBA_BLOCK
    hpm = min(MOBA_HEADS_PER_STEP, moba_heads)
    assert moba_heads % hpm == 0 and nb <= LANES
    slopes = 2.0 ** (-8.0 * (jnp.arange(moba_heads, dtype=F32) + 1.0) / moba_heads)
    qkv_b3 = qkv_b.reshape(b, s, 3 * moba_w)
    mg = moba_heads // hpm
    ob = pl.pallas_call(
        functools.partial(_moba_kernel, nb=nb, hp=hpm),
        grid=(b, mg, nb),
        in_specs=[pl.BlockSpec(memory_space=pltpu.SMEM),
                  pl.BlockSpec((None, MOBA_BLOCK, hpm * HEAD_DIM), lambda i, h, j: (i, j, h)),
                  pl.BlockSpec((None, s, hpm * HEAD_DIM), lambda i, h, j: (i, 0, mg + h)),
                  pl.BlockSpec((None, s, hpm * HEAD_DIM), lambda i, h, j: (i, 0, 2 * mg + h))],
        out_specs=pl.BlockSpec((None, MOBA_BLOCK, hpm * HEAD_DIM), lambda i, h, j: (i, j, h)),
        out_shape=jax.ShapeDtypeStruct((b, s, moba_w), BF16),
        scratch_shapes=[pltpu.VMEM((hpm, LANES, HEAD_DIM), F32)],
        compiler_params=pltpu.CompilerParams(
            dimension_semantics=("parallel", "parallel", "arbitrary"), vmem_limit_bytes=VMEM_LIMIT),
        name="moba",
    )(slopes, qkv_b3, qkv_b3, qkv_b3)
    ob = ob.reshape(t, moba_w)

    w1 = w_in_c[0]
    e1 = 3 * fox_w
    e2 = e1 + fox_heads
    w1_big = jnp.concatenate([w1[:, :fox_w] * ATT_SCALE, w1[:, fox_w:e1], w1[:, e2:]],
                             axis=1).astype(BF16)
    w1_st = w1[:, e1:e2].T
    w1_st = jnp.pad(w1_st, ((0, SMALL_ROWS - fox_heads), (0, 0))).astype(BF16)
    fb = jnp.zeros((SUBLANES, LANES), F32).at[:fox_heads, 0].set(forget_b[0])
    ple_w = lambda i: (w_ple_gate[i].astype(BF16), w_ple_proj[i].astype(BF16))
    wg0, wp0 = ple_w(0)
    wout0 = w_out_ab[0].astype(BF16)
    x2, qkv_c, z1, cfox = pl.pallas_call(
        functools.partial(_rows_d_kernel, tiles_per_seq=s // tm),
        grid=(n_row,),
        in_specs=[_row_spec(tm, gdn_w), _row_spec(tm, moba_w), _row_spec(tm, mix_ab), _row_spec(tm, d),
                  _row_spec(tm, PLE_DIM), _const_spec(wout0.shape), _const_spec((1, d)),
                  _const_spec(wg0.shape), _const_spec(wp0.shape), _const_spec((1, d)),
                  _const_spec(w1_big.shape), _const_spec(w1_st.shape), _const_spec(fb.shape)],
        out_specs=[_row_spec(tm, d), _row_spec(tm, e1), _row_spec(tm, fox_w),
                   pl.BlockSpec((SUBLANES, tm), lambda i: (0, i))],
        out_shape=[jax.ShapeDtypeStruct((t, d), F32), jax.ShapeDtypeStruct((t, e1), BF16),
                   jax.ShapeDtypeStruct((t, fox_w), BF16), jax.ShapeDtypeStruct((SUBLANES, t), F32)],
        scratch_shapes=[pltpu.VMEM((SUBLANES, LANES), F32)],
        compiler_params=_row_params("arbitrary"),
        name="rows_d",
    )(oa, ob, z0, xf, pf[0], wout0, row(ple_norm_g[0]), wg0, wp0, row(norm_g[1]), w1_big, w1_st, fb)

    tq, tk = FOX_TQ, FOX_TK
    hpf = min(FOX_HEADS_PER_STEP, fox_heads)
    assert fox_heads % hpf == 0 and tq % tk == 0
    fg = fox_heads // hpf
    qkv_c3 = qkv_c.reshape(b, s, e1)
    c_tiles = cfox.reshape(SUBLANES, b * (s // tk), tk)
    oc = pl.pallas_call(
        functools.partial(_fox_kernel, hp=hpf, tk=tk),
        grid=(b, fg, s // tq),
        in_specs=[pl.BlockSpec((None, tq, hpf * HEAD_DIM), lambda i, h, j: (i, j, h)),
                  pl.BlockSpec((None, s, hpf * HEAD_DIM), lambda i, h, j: (i, 0, fg + h)),
                  pl.BlockSpec((None, s, hpf * HEAD_DIM), lambda i, h, j: (i, 0, 2 * fg + h)),
                  pl.BlockSpec((SUBLANES, s // tk, tk), lambda i, h, j: (0, i, 0))],
        out_specs=pl.BlockSpec((None, tq, hpf * HEAD_DIM), lambda i, h, j: (i, j, h)),
        out_shape=jax.ShapeDtypeStruct((b, s, fox_w), BF16),
        compiler_params=pltpu.CompilerParams(
            dimension_semantics=("parallel", "parallel", "arbitrary"), vmem_limit_bytes=VMEM_LIMIT),
        name="fox",
    )(qkv_c3, qkv_c3, qkv_c3, c_tiles)
    oc = oc.reshape(t, fox_w)

    wg1, wp1 = ple_w(1)
    wout1 = w_out_c[0].astype(BF16)
    out = pl.pallas_call(
        _rows_f_kernel,
        grid=(n_row,),
        in_specs=[_row_spec(tm, fox_w), _row_spec(tm, fox_w), _row_spec(tm, d), _row_spec(tm, PLE_DIM),
                  _const_spec(wout1.shape), _const_spec((1, d)), _const_spec(wg1.shape),
                  _const_spec(wp1.shape), _const_spec((1, d))],
        out_specs=_row_spec(tm, d),
        out_shape=jax.ShapeDtypeStruct((t, d), F32),
        compiler_params=_row_params("parallel"),
        name="rows_f",
    )(oc, z1, x2, pf[1], wout1, row(ple_norm_g[1]), wg1, wp1, row(final_g))
    return out.reshape(b, s, d)
```

```python
import functools

import jax
import jax.numpy as jnp
from jax import lax
from jax.experimental import pallas as pl
from jax.experimental.pallas import tpu as pltpu

F32 = jnp.float32
BF16 = jnp.bfloat16

HEAD_DIM = 128
PLE_DIM = 256
CONV_K = 4
GDN_CHUNK = 64
MOBA_BLOCK = 256
MOBA_TOPK = 3
RMS_EPS = 1e-6
NEG = -(2.0 ** 100)
LANES = 128
SUBLANES = 8
SMALL_ROWS = 16
VMEM_LIMIT = 56 * 1024 * 1024

LOG2E = 1.4426950408889634
ATT_SCALE = HEAD_DIM ** -0.5 * LOG2E

ROW_TILE = 256
PROJ_CHUNK = 512
GDN_TILE = 256
MOBA_HEADS_PER_STEP = 4
MOBA_TK = 4 * MOBA_BLOCK
FOX_HEADS_PER_STEP = 2
FOX_TQ = 512
FOX_TK = 1024
BIAS_PARTS = 3
SEL_LANE0 = 8


def _sigmoid(x):
    return 1.0 / (1.0 + jnp.exp(-x))


def _silu(x):
    return x * _sigmoid(x)


def _softplus(x):
    return jnp.maximum(x, 0.0) + jnp.log1p(jnp.exp(-jnp.abs(x)))


def _log_sigmoid(x):
    return jnp.minimum(x, 0.0) - jnp.log1p(jnp.exp(-jnp.abs(x)))


def _rms(xf, g):
    return xf * lax.rsqrt(jnp.mean(xf * xf, axis=-1, keepdims=True) + RMS_EPS) * g


def _dot(a, b):
    return jnp.dot(a, b, preferred_element_type=F32)


def _dot_nt(a, b):
    return lax.dot_general(a, b, (((1,), (1,)), ((), ())), preferred_element_type=F32)


def _seg_cumsum(v, seg, axis):
    pos = lax.broadcasted_iota(jnp.int32, v.shape, axis) & (seg - 1)
    s = 1
    while s < seg:
        v = v + jnp.where(pos >= s, pltpu.roll(v, s, axis=axis), 0.0)
        s *= 2
    return v


def _mix_ple(o_parts, z, x, p, w_out_ref, gple, w_gate_ref, w_proj_ref):
    gz = _silu(z.astype(F32))
    acc = x
    off = 0
    for o in o_parts:
        w = o.shape[1]
        y = (o.astype(F32) * gz[:, off:off + w]).astype(BF16)
        acc = acc + _dot(y, w_out_ref[off:off + w, :])
        off += w
    hn = _rms(acc, gple).astype(BF16)
    gate = _sigmoid(_dot(hn, w_gate_ref[...]))
    pp = _dot(p.astype(BF16), w_proj_ref[...])
    return acc + gate * pp


def _project(hn, w_ref, out_refs):
    off = 0
    for o_ref in out_refs:
        n = o_ref.shape[1]
        for c0 in range(0, n, PROJ_CHUNK):
            c1 = min(n, c0 + PROJ_CHUNK)
            o_ref[:, c0:c1] = _dot(hn, w_ref[:, off + c0:off + c1]).astype(o_ref.dtype)
        off += n


def _rows_a_kernel(x_ref, g_ref, w_ref, ws_ref, wst_ref,
                   qkva_ref, qkvb_ref, z_ref, ab_ref, abt_ref):
    hn = _rms(x_ref[...], g_ref[...]).astype(BF16)
    _project(hn, w_ref, (qkva_ref, qkvb_ref, z_ref))
    ab_ref[...] = _dot(hn, ws_ref[...])
    abt_ref[...] = _dot_nt(wst_ref[...], hn)[:SUBLANES]


def _rows_d_kernel(oa_ref, ob_ref, z_ref, x_ref, p_ref, wout_ref, gple_ref, wgate_ref, wproj_ref,
                   g_ref, w_ref, wf_ref, fb_ref,
                   x2_ref, qkv_ref, z1_ref, c_ref, carry_ref, *, tiles_per_seq):
    x2 = _mix_ple((oa_ref[...], ob_ref[...]), z_ref[...], x_ref[...], p_ref[...],
                  wout_ref, gple_ref[...], wgate_ref, wproj_ref)
    x2_ref[...] = x2
    hn = _rms(x2, g_ref[...]).astype(BF16)
    _project(hn, w_ref, (qkv_ref, z1_ref))

    @pl.when(pl.program_id(0) % tiles_per_seq == 0)
    def _():
        carry_ref[...] = jnp.zeros_like(carry_ref)

    log_f = _log_sigmoid(_dot(hn, wf_ref[...]) + fb_ref[...])
    tm = log_f.shape[0]
    c = _seg_cumsum(log_f, tm, 0) + carry_ref[...]
    c_ref[...] = c
    carry_ref[...] = c[tm - 1:tm, :]


def _rows_f_kernel(o_ref, z_ref, x_ref, p_ref, wout_ref, gple_ref, wgate_ref, wproj_ref,
                   g_ref, out_ref):
    x2 = _mix_ple((o_ref[...],), z_ref[...], x_ref[...], p_ref[...],
                  wout_ref, gple_ref[...], wgate_ref, wproj_ref)
    out_ref[...] = _rms(x2, g_ref[...])


def _row_spec(tm, n):
    return pl.BlockSpec((tm, n), lambda i: (i, 0))


def _const_spec(shape):
    return pl.BlockSpec(shape, lambda i: (0,) * len(shape), pipeline_mode=pl.Buffered(1))


def _row_params(semantics):
    return pltpu.CompilerParams(dimension_semantics=(semantics,), vmem_limit_bytes=VMEM_LIMIT)


def _bdot(a, b):
    return lax.dot_general(a, b, (((2,), (1,)), ((0,), (0,))), preferred_element_type=F32)


def _bdot_nt(a, b):
    return lax.dot_general(a, b, (((2,), (2,)), ((0,), (0,))), preferred_element_type=F32)


def _bdot_tn(a, b):
    return lax.dot_general(a, b, (((1,), (1,)), ((0,), (0,))), preferred_element_type=F32)


def _inv_unit_lower(a):
    c = a.shape[-1]
    eye = (lax.broadcasted_iota(jnp.int32, (c, c), 0)
           == lax.broadcasted_iota(jnp.int32, (c, c), 1)).astype(F32)
    t = eye - a
    p = a
    k = 2
    while k < c:
        pb = p.astype(BF16)
        p = _bdot(pb, pb)
        t = t + _bdot(t.astype(BF16), p.astype(BF16))
        k *= 2
    return t


def _gdn_kernel(qkv_ref, ab_ref, abt_ref, cw_ref, hp_ref, hpt_ref, gn_ref,
                o_ref, state_ref, tail_ref, *, heads):
    ts = qkv_ref.shape[0]
    width = heads * HEAD_DIM
    ck = GDN_CHUNK
    nc = ts // ck

    @pl.when(pl.program_id(1) == 0)
    def _():
        state_ref[...] = jnp.zeros_like(state_ref)
        tail_ref[...] = jnp.zeros_like(tail_ref)

    u = qkv_ref[...].astype(F32)
    prev = tail_ref[...]
    tail_ref[...] = u[ts - SUBLANES:, :]
    cw = cw_ref[...]
    row8 = lax.broadcasted_iota(jnp.int32, prev.shape, 0)
    acc = u * cw[CONV_K - 1:CONV_K, :]
    for s in range(1, CONV_K):
        rolled = pltpu.roll(u, s, axis=0)
        first = jnp.where(row8 < s, pltpu.roll(prev, s, axis=0), rolled[:SUBLANES])
        shifted = jnp.concatenate([first, rolled[SUBLANES:]], axis=0)
        acc = acc + shifted * cw[CONV_K - 1 - s:CONV_K - s, :]
    xc = _silu(acc)

    ab = ab_ref[...]
    hp = hp_ref[...]
    g_col = -jnp.exp(hp[0:1, :]) * _softplus(ab + hp[1:2, :])
    beta_col = _sigmoid(ab)
    gcum_col = _seg_cumsum(g_col, ck, 0)
    hpt = hpt_ref[...]
    g_row = -jnp.exp(hpt[:, 0:1]) * _softplus(abt_ref[...] + hpt[:, 1:2])
    gcum_row = _seg_cumsum(g_row, ck, 1)

    def chunks(col0):
        return jnp.concatenate(
            [xc[:, col0 + h * HEAD_DIM:col0 + (h + 1) * HEAD_DIM].reshape(nc, ck, HEAD_DIM)
             for h in range(heads)], axis=0)

    def col_chunks(a, lane0):
        return jnp.concatenate(
            [a[:, lane0 + h:lane0 + h + 1].reshape(nc, ck, 1) for h in range(heads)], axis=0)

    q3, k3, v3 = chunks(0), chunks(width), chunks(2 * width)
    q3 = q3 * lax.rsqrt(jnp.sum(q3 * q3, axis=-1, keepdims=True) + RMS_EPS) * (HEAD_DIM ** -0.5)
    k3 = k3 * lax.rsqrt(jnp.sum(k3 * k3, axis=-1, keepdims=True) + RMS_EPS)
    beta3 = col_chunks(beta_col, heads)
    gc3 = col_chunks(gcum_col, 0)
    gr3 = jnp.stack([gcum_row[h:h + 1, c * ck:(c + 1) * ck]
                     for h in range(heads) for c in range(nc)], axis=0)

    ri = lax.broadcasted_iota(jnp.int32, (ck, ck), 0)
    ci = lax.broadcasted_iota(jnp.int32, (ck, ck), 1)
    incl = ri >= ci
    strict = ri > ci
    decay = jnp.where(incl, jnp.exp(jnp.where(incl, gc3 - gr3, 0.0)), 0.0)
    kb3 = k3 * beta3
    k3b = k3.astype(BF16)
    a_mat = jnp.where(strict, _bdot_nt(kb3.astype(BF16), k3b) * decay, 0.0)
    t_inv = _inv_unit_lower(a_mat)
    exg = jnp.exp(gc3)
    rhs = jnp.concatenate([v3 * beta3, kb3 * exg], axis=2).astype(BF16)
    sol = _bdot(t_inv.astype(BF16), rhs)
    u3, w3 = sol[:, :, :HEAD_DIM], sol[:, :, HEAD_DIM:]
    qk3 = jnp.where(incl, _bdot_nt(q3.astype(BF16), k3b) * decay, 0.0).astype(BF16)
    g_last = gc3[:, ck - 1:ck, :]
    wq3 = jnp.concatenate([w3, q3 * exg], axis=1).astype(BF16)
    kdec3 = (k3 * jnp.exp(g_last - gc3)).astype(BF16)
    gl3 = jnp.exp(g_last)

    by_head = lambda a: a.reshape((heads, nc) + a.shape[1:])
    u4, qk4, wq4, kdec4, gl4 = (by_head(a) for a in (u3, qk3, wq3, kdec3, gl3))
    gn = gn_ref[...]
    state = state_ref[...]
    for c in range(nc):
        ws = _bdot(wq4[:, c], state.astype(BF16))
        v_new = (u4[:, c] - ws[:, :ck]).astype(BF16)
        o = ws[:, ck:] + _bdot(qk4[:, c], v_new)
        state = state * gl4[:, c] + _bdot_tn(kdec4[:, c], v_new)
        o = _rms(o, gn).astype(o_ref.dtype)
        for h in range(heads):
            o_ref[c * ck:(c + 1) * ck, h * HEAD_DIM:(h + 1) * HEAD_DIM] = o[h]
    state_ref[...] = state


def _split_bias(x):
    parts = []
    for _ in range(BIAS_PARTS):
        piece = x.astype(BF16).astype(F32)
        parts.append(piece)
        x = x - piece
    return parts


def _key_ext(bias, extra=None):
    lane = lax.broadcasted_iota(jnp.int32, (bias.shape[0], LANES), 1)
    ext = jnp.zeros((bias.shape[0], LANES), F32) if extra is None else extra
    for i, piece in enumerate(_split_bias(bias)):
        ext = jnp.where(lane == i, piece, ext)
    return ext.astype(BF16)


def _ones_lane0(n):
    lane = lax.broadcasted_iota(jnp.int32, (n, LANES), 1)
    return jnp.where(lane == 0, 1.0, 0.0).astype(BF16)


def _attend_init(qxs):
    tq = qxs[0].shape[0]
    return tuple((jnp.full((tq, 1), NEG, F32), jnp.zeros((tq, HEAD_DIM + LANES), F32)) for _ in qxs)


def _attend_tile(qxs, kx_ref, vx_ref, tk, t, mask, carries):
    heads = range(len(qxs))
    k0 = pl.multiple_of(t * tk, tk)
    ss = [_dot_nt(qxs[h], kx_ref[h, pl.ds(k0, tk), :]) for h in heads]
    if mask is not None:
        ss = [jnp.where(mask, s, NEG) for s in ss]
    ms = [jnp.maximum(m, jnp.max(s, axis=1, keepdims=True)) for s, (m, _) in zip(ss, carries)]
    out = []
    for h in heads:
        m, accx = carries[h]
        pv = _dot(jnp.exp2(ss[h] - ms[h]).astype(BF16), vx_ref[h, pl.ds(k0, tk), :])
        out.append((ms[h], jnp.exp2(m - ms[h]) * accx + pv))
    return tuple(out)


def _attend_range(qxs, kx_ref, vx_ref, tk, t_lo, t_hi, carries):
    return lax.fori_loop(t_lo, t_hi,
                         lambda t, cr: _attend_tile(qxs, kx_ref, vx_ref, tk, t, None, cr), carries)


def _softmax_out(carry, dtype):
    _, accx = carry
    return (accx[:, :HEAD_DIM] / accx[:, HEAD_DIM:HEAD_DIM + 1]).astype(dtype)


def _head_cols(hh):
    return slice(hh * HEAD_DIM, (hh + 1) * HEAD_DIM)


def _moba_kernel(slopes_ref, q_ref, k_ref, v_ref, o_ref, kx_ref, vx_ref, kmean_ref, *, nb, hp, tk):
    blk = MOBA_BLOCK
    s_len = k_ref.shape[0]
    hg = pl.program_id(1)
    qi = pl.program_id(2)
    nbp = -(-nb // SUBLANES) * SUBLANES
    bpt = tk // blk

    @pl.when(qi == 0)
    def _():
        pos = lax.broadcasted_iota(jnp.int32, (s_len, 1), 0)
        lane = lax.broadcasted_iota(jnp.int32, (s_len, LANES), 1)
        onehot = jnp.where(lane - SEL_LANE0 == lax.broadcasted_iota(jnp.int32, (s_len, LANES), 0) // blk,
                           1.0, 0.0)
        ones = _ones_lane0(s_len)
        kmean_ref[...] = jnp.zeros_like(kmean_ref)
        for hh in range(hp):
            slope2 = slopes_ref[hg * hp + hh] * LOG2E
            kx_ref[hh, :, :HEAD_DIM] = k_ref[:, _head_cols(hh)]
            kx_ref[hh, :, HEAD_DIM:] = _key_ext(slope2 * pos.astype(F32), onehot)
            vx_ref[hh, :, :HEAD_DIM] = v_ref[:, _head_cols(hh)]
            vx_ref[hh, :, HEAD_DIM:] = ones
            for j in range(nb):
                kmean_ref[hh, j:j + 1, :] = jnp.mean(
                    k_ref[j * blk:(j + 1) * blk, _head_cols(hh)].astype(F32), axis=0, keepdims=True)

    blk_id = lax.broadcasted_iota(jnp.int32, (nbp, blk), 0)
    past = blk_id < qi
    lane = lax.broadcasted_iota(jnp.int32, (blk, LANES), 1)
    qxs = []
    for hh in range(hp):
        q = q_ref[:, _head_cols(hh)]
        gate = lax.dot_general(kmean_ref[hh], q.astype(F32), (((1,), (1,)), ((), ())),
                               precision=lax.Precision.HIGHEST, preferred_element_type=F32)[:nbp]
        gm = jnp.where(past, gate, NEG)
        rank = jnp.zeros(gm.shape, jnp.int32)
        for j in range(nb):
            cj = gm[j:j + 1, :]
            beats = (cj > gm) | ((cj == gm) & (blk_id > j))
            rank = rank + beats.astype(jnp.int32)
        sel_t = jnp.where(((rank < MOBA_TOPK) & past) | (blk_id == qi), 0.0, NEG)
        sel_t = jnp.concatenate([jnp.zeros((SEL_LANE0, blk), F32), sel_t,
                                 jnp.zeros((LANES - SEL_LANE0 - nbp, blk), F32)], axis=0)
        qext = jnp.where(lane < BIAS_PARTS, 1.0, sel_t.T).astype(BF16)
        qxs.append(jnp.concatenate([q, qext], axis=1))

    n_past = qi // bpt
    r = lax.broadcasted_iota(jnp.int32, (blk, tk), 0)
    c = lax.broadcasted_iota(jnp.int32, (blk, tk), 1)
    carries = _attend_range(qxs, kx_ref, vx_ref, tk, 0, n_past, _attend_init(qxs))
    carries = _attend_tile(qxs, kx_ref, vx_ref, tk, n_past, r + (qi - n_past * bpt) * blk >= c, carries)
    for hh in range(hp):
        o_ref[:, _head_cols(hh)] = _softmax_out(carries[hh], o_ref.dtype)


def _fox_kernel(q_ref, k_ref, v_ref, c_ref, o_ref, kx_ref, vx_ref, *, hp, tk):
    tq = q_ref.shape[0]
    s_len = k_ref.shape[0]
    hg = pl.program_id(1)
    qi = pl.program_id(2)

    @pl.when(qi == 0)
    def _():
        lane = lax.broadcasted_iota(jnp.int32, (s_len, LANES), 1)
        ones = _ones_lane0(s_len)
        for hh in range(hp):
            c_col = jnp.sum(jnp.where(lane == hg * hp + hh, c_ref[...], 0.0), axis=1, keepdims=True)
            kx_ref[hh, :, :HEAD_DIM] = k_ref[:, _head_cols(hh)]
            kx_ref[hh, :, HEAD_DIM:] = _key_ext(-LOG2E * c_col)
            vx_ref[hh, :, :HEAD_DIM] = v_ref[:, _head_cols(hh)]
            vx_ref[hh, :, HEAD_DIM:] = ones

    lane = lax.broadcasted_iota(jnp.int32, (tq, LANES), 1)
    qext = jnp.where(lane < BIAS_PARTS, 1.0, 0.0).astype(BF16)
    qxs = [jnp.concatenate([q_ref[:, _head_cols(hh)], qext], axis=1) for hh in range(hp)]

    n_past = (qi * tq) // tk
    r = lax.broadcasted_iota(jnp.int32, (tq, tk), 0)
    c = lax.broadcasted_iota(jnp.int32, (tq, tk), 1)
    carries = _attend_range(qxs, kx_ref, vx_ref, tk, 0, n_past, _attend_init(qxs))
    carries = _attend_tile(qxs, kx_ref, vx_ref, tk, n_past, r + (qi * tq - n_past * tk) >= c, carries)
    for hh in range(hp):
        o_ref[:, _head_cols(hh)] = _softmax_out(carries[hh], o_ref.dtype)


def _pad_cols(w, n):
    return jnp.pad(w, ((0, 0), (0, n - w.shape[1])))


def kernel(x, p, norm_g, w_in_ab, conv_w, a_log, dt_bias, gdn_norm_g, w_out_ab, w_in_c, forget_b,
           w_out_c, ple_norm_g, w_ple_gate, w_ple_proj, final_g):
    b, s, d = x.shape
    t = b * s
    gdn_heads = a_log.shape[1]
    gdn_w = gdn_heads * HEAD_DIM
    moba_w = (w_out_ab.shape[1] - gdn_w)
    moba_heads = moba_w // HEAD_DIM
    fox_heads = forget_b.shape[1]
    fox_w = fox_heads * HEAD_DIM
    mix_ab = gdn_w + moba_w
    assert s % MOBA_BLOCK == 0 and s % ROW_TILE == 0 and s % GDN_TILE == 0 and s % FOX_TQ == 0
    assert 2 * gdn_heads <= SUBLANES and fox_heads <= LANES
    tm = ROW_TILE
    n_row = t // tm

    xf = x.reshape(t, d)
    pf = p.reshape(p.shape[0], t, PLE_DIM)
    row = lambda v: v.reshape(1, -1).astype(F32)

    w0 = w_in_ab[0]
    c1 = 3 * gdn_w
    c2 = c1 + 2 * gdn_heads
    w0_big = jnp.concatenate([w0[:, :c1], w0[:, c2:c2 + moba_w] * ATT_SCALE, w0[:, c2 + moba_w:]],
                             axis=1).astype(BF16)
    w0_small = w0[:, c1:c2]
    w0_s = _pad_cols(w0_small, LANES).astype(BF16)
    w0_st = jnp.pad(w0_small.T, ((0, SMALL_ROWS - 2 * gdn_heads), (0, 0))).astype(BF16)

    qkv_a, qkv_b, z0, ab, abt = pl.pallas_call(
        _rows_a_kernel,
        grid=(n_row,),
        in_specs=[_row_spec(tm, d), _const_spec((1, d)), _const_spec(w0_big.shape),
                  _const_spec(w0_s.shape), _const_spec(w0_st.shape)],
        out_specs=[_row_spec(tm, c1), _row_spec(tm, 3 * moba_w), _row_spec(tm, mix_ab),
                   _row_spec(tm, LANES), pl.BlockSpec((SUBLANES, tm), lambda i: (0, i))],
        out_shape=[jax.ShapeDtypeStruct((t, c1), BF16), jax.ShapeDtypeStruct((t, 3 * moba_w), BF16),
                   jax.ShapeDtypeStruct((t, mix_ab), BF16), jax.ShapeDtypeStruct((t, LANES), F32),
                   jax.ShapeDtypeStruct((SUBLANES, t), F32)],
        compiler_params=_row_params("parallel"),
        name="rows_a",
    )(xf, row(norm_g[0]), w0_big, w0_s, w0_st)

    ts = GDN_TILE
    n_seq = s // ts
    hp = jnp.zeros((SUBLANES, LANES), F32)
    hp = hp.at[0, :gdn_heads].set(a_log[0]).at[1, :gdn_heads].set(dt_bias[0])
    hpt = jnp.zeros((SUBLANES, LANES), F32)
    hpt = hpt.at[:gdn_heads, 0].set(a_log[0]).at[:gdn_heads, 1].set(dt_bias[0])
    oa = pl.pallas_call(
        functools.partial(_gdn_kernel, heads=gdn_heads),
        grid=(b, n_seq),
        in_specs=[pl.BlockSpec((ts, c1), lambda i, j: (i * n_seq + j, 0)),
                  pl.BlockSpec((ts, LANES), lambda i, j: (i * n_seq + j, 0)),
                  pl.BlockSpec((SUBLANES, ts), lambda i, j: (0, i * n_seq + j)),
                  pl.BlockSpec((CONV_K, c1), lambda i, j: (0, 0)),
                  pl.BlockSpec((SUBLANES, LANES), lambda i, j: (0, 0)),
                  pl.BlockSpec((SUBLANES, LANES), lambda i, j: (0, 0)),
                  pl.BlockSpec((1, HEAD_DIM), lambda i, j: (0, 0))],
        out_specs=pl.BlockSpec((ts, gdn_w), lambda i, j: (i * n_seq + j, 0)),
        out_shape=jax.ShapeDtypeStruct((t, gdn_w), BF16),
        scratch_shapes=[pltpu.VMEM((gdn_heads, HEAD_DIM, HEAD_DIM), F32),
                        pltpu.VMEM((SUBLANES, c1), F32)],
        compiler_params=pltpu.CompilerParams(dimension_semantics=("parallel", "arbitrary"),
                                             vmem_limit_bytes=VMEM_LIMIT),
        name="gdn",
    )(qkv_a, ab, abt, conv_w[0].astype(F32), hp, hpt, row(gdn_norm_g[0]))

    nb = s // MOBA_BLOCK
    hpm = min(MOBA_HEADS_PER_STEP, moba_heads)
    assert moba_heads % hpm == 0 and SEL_LANE0 + nb <= LANES and s % MOBA_TK == 0
    slopes = 2.0 ** (-8.0 * (jnp.arange(moba_heads, dtype=F32) + 1.0) / moba_heads)
    qkv_b3 = qkv_b.reshape(b, s, 3 * moba_w)
    mg = moba_heads // hpm
    ob = pl.pallas_call(
        functools.partial(_moba_kernel, nb=nb, hp=hpm, tk=MOBA_TK),
        grid=(b, mg, nb),
        in_specs=[pl.BlockSpec(memory_space=pltpu.SMEM),
                  pl.BlockSpec((None, MOBA_BLOCK, hpm * HEAD_DIM), lambda i, h, j: (i, j, h)),
                  pl.BlockSpec((None, s, hpm * HEAD_DIM), lambda i, h, j: (i, 0, mg + h)),
                  pl.BlockSpec((None, s, hpm * HEAD_DIM), lambda i, h, j: (i, 0, 2 * mg + h))],
        out_specs=pl.BlockSpec((None, MOBA_BLOCK, hpm * HEAD_DIM), lambda i, h, j: (i, j, h)),
        out_shape=jax.ShapeDtypeStruct((b, s, moba_w), BF16),
        scratch_shapes=[pltpu.VMEM((hpm, s, HEAD_DIM + LANES), BF16),
                        pltpu.VMEM((hpm, s, HEAD_DIM + LANES), BF16),
                        pltpu.VMEM((hpm, LANES, HEAD_DIM), F32)],
        compiler_params=pltpu.CompilerParams(
            dimension_semantics=("parallel", "parallel", "arbitrary"), vmem_limit_bytes=VMEM_LIMIT),
        name="moba",
    )(slopes, qkv_b3, qkv_b3, qkv_b3)
    ob = ob.reshape(t, moba_w)

    w1 = w_in_c[0]
    e1 = 3 * fox_w
    e2 = e1 + fox_heads
    w1_big = jnp.concatenate([w1[:, :fox_w] * ATT_SCALE, w1[:, fox_w:e1], w1[:, e2:]],
                             axis=1).astype(BF16)
    w1_f = _pad_cols(w1[:, e1:e2], LANES).astype(BF16)
    fb = _pad_cols(row(forget_b[0]), LANES)
    ple_w = lambda i: (w_ple_gate[i].astype(BF16), w_ple_proj[i].astype(BF16))
    wg0, wp0 = ple_w(0)
    wout0 = w_out_ab[0].astype(BF16)
    x2, qkv_c, z1, cfox = pl.pallas_call(
        functools.partial(_rows_d_kernel, tiles_per_seq=s // tm),
        grid=(n_row,),
        in_specs=[_row_spec(tm, gdn_w), _row_spec(tm, moba_w), _row_spec(tm, mix_ab), _row_spec(tm, d),
                  _row_spec(tm, PLE_DIM), _const_spec(wout0.shape), _const_spec((1, d)),
                  _const_spec(wg0.shape), _const_spec(wp0.shape), _const_spec((1, d)),
                  _const_spec(w1_big.shape), _const_spec(w1_f.shape), _const_spec(fb.shape)],
        out_specs=[_row_spec(tm, d), _row_spec(tm, e1), _row_spec(tm, fox_w), _row_spec(tm, LANES)],
        out_shape=[jax.ShapeDtypeStruct((t, d), F32), jax.ShapeDtypeStruct((t, e1), BF16),
                   jax.ShapeDtypeStruct((t, fox_w), BF16), jax.ShapeDtypeStruct((t, LANES), F32)],
        scratch_shapes=[pltpu.VMEM((1, LANES), F32)],
        compiler_params=_row_params("arbitrary"),
        name="rows_d",
    )(oa, ob, z0, xf, pf[0], wout0, row(ple_norm_g[0]), wg0, wp0, row(norm_g[1]), w1_big, w1_f, fb)

    tq, tk = FOX_TQ, FOX_TK
    hpf = min(FOX_HEADS_PER_STEP, fox_heads)
    assert fox_heads % hpf == 0 and tk % tq == 0 and s % tk == 0
    fg = fox_heads // hpf
    qkv_c3 = qkv_c.reshape(b, s, e1)
    oc = pl.pallas_call(
        functools.partial(_fox_kernel, hp=hpf, tk=tk),
        grid=(b, fg, s // tq),
        in_specs=[pl.BlockSpec((None, tq, hpf * HEAD_DIM), lambda i, h, j: (i, j, h)),
                  pl.BlockSpec((None, s, hpf * HEAD_DIM), lambda i, h, j: (i, 0, fg + h)),
                  pl.BlockSpec((None, s, hpf * HEAD_DIM), lambda i, h, j: (i, 0, 2 * fg + h)),
                  pl.BlockSpec((s, LANES), lambda i, h, j: (i, 0))],
        out_specs=pl.BlockSpec((None, tq, hpf * HEAD_DIM), lambda i, h, j: (i, j, h)),
        out_shape=jax.ShapeDtypeStruct((b, s, fox_w), BF16),
        scratch_shapes=[pltpu.VMEM((hpf, s, HEAD_DIM + LANES), BF16),
                        pltpu.VMEM((hpf, s, HEAD_DIM + LANES), BF16)],
        compiler_params=pltpu.CompilerParams(
            dimension_semantics=("parallel", "parallel", "arbitrary"), vmem_limit_bytes=VMEM_LIMIT),
        name="fox",
    )(qkv_c3, qkv_c3, qkv_c3, cfox)
    oc = oc.reshape(t, fox_w)

    wg1, wp1 = ple_w(1)
    wout1 = w_out_c[0].astype(BF16)
    out = pl.pallas_call(
        _rows_f_kernel,
        grid=(n_row,),
        in_specs=[_row_spec(tm, fox_w), _row_spec(tm, fox_w), _row_spec(tm, d), _row_spec(tm, PLE_DIM),
                  _const_spec(wout1.shape), _const_spec((1, d)), _const_spec(wg1.shape),
                  _const_spec(wp1.shape), _const_spec((1, d))],
        out_specs=_row_spec(tm, d),
        out_shape=jax.ShapeDtypeStruct((t, d), F32),
        compiler_params=_row_params("parallel"),
        name="rows_f",
    )(oc, z1, x2, pf[1], wout1, row(ple_norm_g[1]), wg1, wp1, row(final_g))
    return out.reshape(b, s, d)
```

```python
import functools

import jax
import jax.numpy as jnp
from jax import lax
from jax.experimental import pallas as pl
from jax.experimental.pallas import tpu as pltpu

F32 = jnp.float32
BF16 = jnp.bfloat16

HEAD_DIM = 128
PLE_DIM = 256
CONV_K = 4
GDN_CHUNK = 64
MOBA_BLOCK = 256
MOBA_TOPK = 3
RMS_EPS = 1e-6
NEG = -(2.0 ** 100)
LANES = 128
SUBLANES = 8
SMALL_ROWS = 16
VMEM_LIMIT = 56 * 1024 * 1024

LOG2E = 1.4426950408889634
ATT_SCALE = HEAD_DIM ** -0.5 * LOG2E

ROW_TILE = 512
PROJ_CHUNK = 512
GDN_TILE = 256
GDN_SEQS_PER_STEP = 2
MOBA_HEADS_PER_STEP = 4
MOBA_TK = 4 * MOBA_BLOCK
FOX_HEADS_PER_STEP = 2
FOX_TQ = 1024
FOX_TK = 1024
BIAS_PARTS = 3
SEL_LANE0 = 8


def _sigmoid(x):
    return 1.0 / (1.0 + jnp.exp(-x))


def _silu(x):
    return x * _sigmoid(x)


def _softplus(x):
    return jnp.maximum(x, 0.0) + jnp.log1p(jnp.exp(-jnp.abs(x)))


def _log_sigmoid(x):
    return jnp.minimum(x, 0.0) - jnp.log1p(jnp.exp(-jnp.abs(x)))


def _rms(xf, g):
    return xf * lax.rsqrt(jnp.mean(xf * xf, axis=-1, keepdims=True) + RMS_EPS) * g


def _dot(a, b):
    return jnp.dot(a, b, preferred_element_type=F32)


def _dot_nt(a, b):
    return lax.dot_general(a, b, (((1,), (1,)), ((), ())), preferred_element_type=F32)


def _seg_cumsum(v, seg, axis):
    pos = lax.broadcasted_iota(jnp.int32, v.shape, axis) & (seg - 1)
    s = 1
    while s < seg:
        v = v + jnp.where(pos >= s, pltpu.roll(v, s, axis=axis), 0.0)
        s *= 2
    return v


def _mix_ple(o_parts, z, x, p, w_out_ref, gple, w_gate_ref, w_proj_ref):
    gz = _silu(z.astype(F32))
    acc = x
    off = 0
    for o in o_parts:
        w = o.shape[1]
        y = (o.astype(F32) * gz[:, off:off + w]).astype(BF16)
        acc = acc + _dot(y, w_out_ref[off:off + w, :])
        off += w
    hn = _rms(acc, gple).astype(BF16)
    gate = _sigmoid(_dot(hn, w_gate_ref[...]))
    pp = _dot(p.astype(BF16), w_proj_ref[...])
    return acc + gate * pp


def _project(hn, w_ref, out_refs):
    off = 0
    for o_ref in out_refs:
        n = o_ref.shape[1]
        for c0 in range(0, n, PROJ_CHUNK):
            c1 = min(n, c0 + PROJ_CHUNK)
            o_ref[:, c0:c1] = _dot(hn, w_ref[:, off + c0:off + c1]).astype(o_ref.dtype)
        off += n


def _rows_a_kernel(x_ref, g_ref, w_ref, ws_ref, wst_ref,
                   qkva_ref, qkvb_ref, z_ref, ab_ref, abt_ref):
    hn = _rms(x_ref[...], g_ref[...]).astype(BF16)
    _project(hn, w_ref, (qkva_ref, qkvb_ref, z_ref))
    ab_ref[...] = _dot(hn, ws_ref[...])
    abt_ref[...] = _dot_nt(wst_ref[...], hn)[:SUBLANES]


def _rows_d_kernel(oa_ref, ob_ref, z_ref, x_ref, p_ref, wout_ref, gple_ref, wgate_ref, wproj_ref,
                   g_ref, w_ref, wf_ref, fb_ref,
                   x2_ref, qkv_ref, z1_ref, c_ref, carry_ref, *, tiles_per_seq):
    x2 = _mix_ple((oa_ref[...], ob_ref[...]), z_ref[...], x_ref[...], p_ref[...],
                  wout_ref, gple_ref[...], wgate_ref, wproj_ref)
    x2_ref[...] = x2
    hn = _rms(x2, g_ref[...]).astype(BF16)
    _project(hn, w_ref, (qkv_ref, z1_ref))

    @pl.when(pl.program_id(0) % tiles_per_seq == 0)
    def _():
        carry_ref[...] = jnp.zeros_like(carry_ref)

    log_f = _log_sigmoid(_dot(hn, wf_ref[...]) + fb_ref[...])
    tm = log_f.shape[0]
    c = _seg_cumsum(log_f, tm, 0) + carry_ref[...]
    c_ref[...] = c
    carry_ref[...] = c[tm - 1:tm, :]


def _rows_f_kernel(o_ref, z_ref, x_ref, p_ref, wout_ref, gple_ref, wgate_ref, wproj_ref,
                   g_ref, out_ref):
    x2 = _mix_ple((o_ref[...],), z_ref[...], x_ref[...], p_ref[...],
                  wout_ref, gple_ref[...], wgate_ref, wproj_ref)
    out_ref[...] = _rms(x2, g_ref[...])


def _row_spec(tm, n):
    return pl.BlockSpec((tm, n), lambda i: (i, 0))


def _layer_spec(tm, n, layer):
    return pl.BlockSpec((None, tm, n), lambda i: (layer, i, 0))


def _const_spec(shape):
    return pl.BlockSpec(shape, lambda i: (0,) * len(shape), pipeline_mode=pl.Buffered(1))


def _row_params(semantics):
    return pltpu.CompilerParams(dimension_semantics=(semantics,), vmem_limit_bytes=VMEM_LIMIT)


def _bdot(a, b):
    return lax.dot_general(a, b, (((2,), (1,)), ((0,), (0,))), preferred_element_type=F32)


def _bdot_nt(a, b):
    return lax.dot_general(a, b, (((2,), (2,)), ((0,), (0,))), preferred_element_type=F32)


def _bdot_tn(a, b):
    return lax.dot_general(a, b, (((1,), (1,)), ((0,), (0,))), preferred_element_type=F32)


def _inv_unit_lower(a):
    c = a.shape[-1]
    eye = (lax.broadcasted_iota(jnp.int32, (c, c), 0)
           == lax.broadcasted_iota(jnp.int32, (c, c), 1)).astype(F32)
    t = eye - a
    p = a
    k = 2
    while k < c:
        pb = p.astype(BF16)
        p = _bdot(pb, pb)
        t = t + _bdot(t.astype(BF16), p.astype(BF16))
        k *= 2
    return t


def _gdn_kernel(qkv_ref, ab_ref, abt_ref, cw_ref, hp_ref, hpt_ref, gn_ref,
                o_ref, state_ref, tail_ref, *, heads):
    nseq, ts = qkv_ref.shape[0], qkv_ref.shape[1]
    width = heads * HEAD_DIM
    ck = GDN_CHUNK
    nc = ts // ck

    @pl.when(pl.program_id(1) == 0)
    def _():
        state_ref[...] = jnp.zeros_like(state_ref)
        tail_ref[...] = jnp.zeros_like(tail_ref)

    cw = cw_ref[...]
    hp = hp_ref[...]
    hpt = hpt_ref[...]
    xcs, beta_cols, gcum_cols, gcum_rows = [], [], [], []
    for sq in range(nseq):
        u = qkv_ref[sq].astype(F32)
        tail_ref[sq, SUBLANES:, :] = u
        acc = u * cw[CONV_K - 1:CONV_K, :]
        for s in range(1, CONV_K):
            acc = acc + (tail_ref[sq, SUBLANES - s:SUBLANES - s + ts, :]
                         * cw[CONV_K - 1 - s:CONV_K - s, :])
        tail_ref[sq, :SUBLANES, :] = u[ts - SUBLANES:, :]
        xcs.append(_silu(acc))
        ab = ab_ref[sq]
        g_col = -jnp.exp(hp[0:1, :]) * _softplus(ab + hp[1:2, :])
        beta_cols.append(_sigmoid(ab))
        gcum_cols.append(_seg_cumsum(g_col, ck, 0))
        g_row = -jnp.exp(hpt[:, 0:1]) * _softplus(abt_ref[sq] + hpt[:, 1:2])
        gcum_rows.append(_seg_cumsum(g_row, ck, 1))

    groups = [(h, sq) for h in range(heads) for sq in range(nseq)]
    ng = len(groups)

    def chunks(col0):
        return jnp.concatenate(
            [xcs[sq][:, col0 + h * HEAD_DIM:col0 + (h + 1) * HEAD_DIM].reshape(nc, ck, HEAD_DIM)
             for h, sq in groups], axis=0)

    def col_chunks(cols, lane0):
        return jnp.concatenate(
            [cols[sq][:, lane0 + h:lane0 + h + 1].reshape(nc, ck, 1) for h, sq in groups], axis=0)

    q3, k3, v3 = chunks(0), chunks(width), chunks(2 * width)
    q3 = q3 * lax.rsqrt(jnp.sum(q3 * q3, axis=-1, keepdims=True) + RMS_EPS) * (HEAD_DIM ** -0.5)
    k3 = k3 * lax.rsqrt(jnp.sum(k3 * k3, axis=-1, keepdims=True) + RMS_EPS)
    beta3 = col_chunks(beta_cols, heads)
    gc3 = col_chunks(gcum_cols, 0)
    gr3 = jnp.stack([gcum_rows[sq][h:h + 1, c * ck:(c + 1) * ck]
                     for h, sq in groups for c in range(nc)], axis=0)

    ri = lax.broadcasted_iota(jnp.int32, (ck, ck), 0)
    ci = lax.broadcasted_iota(jnp.int32, (ck, ck), 1)
    incl = ri >= ci
    strict = ri > ci
    decay = jnp.where(incl, jnp.exp(jnp.where(incl, gc3 - gr3, 0.0)), 0.0)
    kb3 = k3 * beta3
    k3b = k3.astype(BF16)
    a_mat = jnp.where(strict, _bdot_nt(kb3.astype(BF16), k3b) * decay, 0.0)
    t_inv = _inv_unit_lower(a_mat)
    exg = jnp.exp(gc3)
    rhs = jnp.concatenate([v3 * beta3, kb3 * exg], axis=2).astype(BF16)
    sol = _bdot(t_inv.astype(BF16), rhs)
    u3, w3 = sol[:, :, :HEAD_DIM], sol[:, :, HEAD_DIM:]
    qk3 = jnp.where(incl, _bdot_nt(q3.astype(BF16), k3b) * decay, 0.0).astype(BF16)
    g_last = gc3[:, ck - 1:ck, :]
    wq3 = jnp.concatenate([w3, q3 * exg], axis=1).astype(BF16)
    kdec3 = (k3 * jnp.exp(g_last - gc3)).astype(BF16)
    gl3 = jnp.exp(g_last)

    by_group = lambda a: a.reshape((ng, nc) + a.shape[1:])
    u4, qk4, wq4, kdec4, gl4 = (by_group(a) for a in (u3, qk3, wq3, kdec3, gl3))
    gn = gn_ref[...]
    state = state_ref[...]
    for c in range(nc):
        ws = _bdot(wq4[:, c], state.astype(BF16))
        v_new = (u4[:, c] - ws[:, :ck]).astype(BF16)
        o = ws[:, ck:] + _bdot(qk4[:, c], v_new)
        state = state * gl4[:, c] + _bdot_tn(kdec4[:, c], v_new)
        o = _rms(o, gn).astype(o_ref.dtype)
        for gi, (h, sq) in enumerate(groups):
            o_ref[sq, c * ck:(c + 1) * ck, h * HEAD_DIM:(h + 1) * HEAD_DIM] = o[gi]
    state_ref[...] = state


def _split_bias(x):
    parts = []
    for _ in range(BIAS_PARTS):
        piece = x.astype(BF16).astype(F32)
        parts.append(piece)
        x = x - piece
    return parts


def _key_ext(bias, extra=None):
    lane = lax.broadcasted_iota(jnp.int32, (bias.shape[0], LANES), 1)
    ext = jnp.zeros((bias.shape[0], LANES), F32) if extra is None else extra
    for i, piece in enumerate(_split_bias(bias)):
        ext = jnp.where(lane == i, piece, ext)
    return ext.astype(BF16)


def _ones_lane0(n):
    lane = lax.broadcasted_iota(jnp.int32, (n, LANES), 1)
    return jnp.where(lane == 0, 1.0, 0.0).astype(BF16)


def _attend_init(qxs):
    tq = qxs[0].shape[0]
    return tuple((jnp.full((tq, 1), NEG, F32), jnp.zeros((tq, HEAD_DIM + LANES), F32)) for _ in qxs)


def _attend_tile(qxs, kx_ref, vx_ref, tk, k0, mask, carries):
    heads = range(len(qxs))
    ss = [_dot_nt(qxs[h], kx_ref[h, pl.ds(k0, tk), :]) for h in heads]
    if mask is not None:
        ss = [jnp.where(mask, s, NEG) for s in ss]
    ms = [jnp.maximum(m, jnp.max(s, axis=1, keepdims=True)) for s, (m, _) in zip(ss, carries)]
    out = []
    for h in heads:
        m, accx = carries[h]
        pv = _dot(jnp.exp2(ss[h] - ms[h]).astype(BF16), vx_ref[h, pl.ds(k0, tk), :])
        out.append((ms[h], jnp.exp2(m - ms[h]) * accx + pv))
    return tuple(out)


def _attend_causal(qxs, kx_ref, vx_ref, tk, qi):
    tq = qxs[0].shape[0]
    n_past = (qi * tq) // tk

    def tile(t, mask, carries):
        return _attend_tile(qxs, kx_ref, vx_ref, tk, pl.multiple_of(t * tk, tk), mask, carries)

    carries = lax.fori_loop(0, n_past, lambda t, cr: tile(t, None, cr), _attend_init(qxs))
    mask = (lax.broadcasted_iota(jnp.int32, (tq, tk), 0) + (qi * tq - n_past * tk)
            >= lax.broadcasted_iota(jnp.int32, (tq, tk), 1))
    return tile(n_past, mask, carries)


def _softmax_out(carry, dtype):
    _, accx = carry
    return (accx[:, :HEAD_DIM] / accx[:, HEAD_DIM:HEAD_DIM + 1]).astype(dtype)


def _head_cols(hh):
    return slice(hh * HEAD_DIM, (hh + 1) * HEAD_DIM)


def _moba_kernel(slopes_ref, q_ref, k_ref, v_ref, o_ref, kx_ref, vx_ref, kmean_ref, *, nb, hp, tk):
    blk = MOBA_BLOCK
    s_len = k_ref.shape[0]
    hg = pl.program_id(1)
    qi = pl.program_id(2)
    nbp = -(-nb // SUBLANES) * SUBLANES

    @pl.when(qi == 0)
    def _():
        pos = lax.broadcasted_iota(jnp.int32, (s_len, 1), 0)
        lane = lax.broadcasted_iota(jnp.int32, (s_len, LANES), 1)
        onehot = jnp.where(lane - SEL_LANE0 == lax.broadcasted_iota(jnp.int32, (s_len, LANES), 0) // blk,
                           1.0, 0.0)
        ones = _ones_lane0(s_len)
        kmean_ref[...] = jnp.zeros_like(kmean_ref)
        for hh in range(hp):
            slope2 = slopes_ref[hg * hp + hh] * LOG2E
            kx_ref[hh, :, :HEAD_DIM] = k_ref[:, _head_cols(hh)]
            kx_ref[hh, :, HEAD_DIM:] = _key_ext(slope2 * pos.astype(F32), onehot)
            vx_ref[hh, :, :HEAD_DIM] = v_ref[:, _head_cols(hh)]
            vx_ref[hh, :, HEAD_DIM:] = ones
            for j in range(nb):
                kmean_ref[hh, j:j + 1, :] = jnp.mean(
                    k_ref[j * blk:(j + 1) * blk, _head_cols(hh)].astype(F32), axis=0, keepdims=True)

    blk_id = lax.broadcasted_iota(jnp.int32, (nbp, blk), 0)
    past = blk_id < qi
    lane = lax.broadcasted_iota(jnp.int32, (blk, LANES), 1)
    qxs = []
    for hh in range(hp):
        q = q_ref[:, _head_cols(hh)]
        gate = sum(_dot_nt(piece.astype(BF16), q) for piece in _split_bias(kmean_ref[hh, :nbp, :]))
        gm = jnp.where(past, gate, NEG)
        rank = jnp.zeros(gm.shape, jnp.int32)
        for j in range(nb):
            cj = gm[j:j + 1, :]
            beats = (cj > gm) | ((cj == gm) & (blk_id > j))
            rank = rank + beats.astype(jnp.int32)
        sel_t = jnp.where(((rank < MOBA_TOPK) & past) | (blk_id == qi), 0.0, NEG)
        sel_t = jnp.concatenate([jnp.zeros((SEL_LANE0, blk), F32), sel_t,
                                 jnp.zeros((LANES - SEL_LANE0 - nbp, blk), F32)], axis=0)
        qext = jnp.where(lane < BIAS_PARTS, 1.0, sel_t.T).astype(BF16)
        qxs.append(jnp.concatenate([q, qext], axis=1))

    carries = _attend_causal(qxs, kx_ref, vx_ref, tk, qi)
    for hh in range(hp):
        o_ref[:, _head_cols(hh)] = _softmax_out(carries[hh], o_ref.dtype)


def _fox_kernel(q_ref, k_ref, v_ref, c_ref, o_ref, kx_ref, vx_ref, *, hp, tk):
    tq = q_ref.shape[0]
    s_len = k_ref.shape[0]
    hg = pl.program_id(1)
    qi = pl.program_id(2)

    @pl.when(qi == 0)
    def _():
        lane = lax.broadcasted_iota(jnp.int32, (s_len, LANES), 1)
        ones = _ones_lane0(s_len)
        for hh in range(hp):
            c_col = jnp.sum(jnp.where(lane == hg * hp + hh, c_ref[...], 0.0), axis=1, keepdims=True)
            kx_ref[hh, :, :HEAD_DIM] = k_ref[:, _head_cols(hh)]
            kx_ref[hh, :, HEAD_DIM:] = _key_ext(-LOG2E * c_col)
            vx_ref[hh, :, :HEAD_DIM] = v_ref[:, _head_cols(hh)]
            vx_ref[hh, :, HEAD_DIM:] = ones

    lane = lax.broadcasted_iota(jnp.int32, (tq, LANES), 1)
    qext = jnp.where(lane < BIAS_PARTS, 1.0, 0.0).astype(BF16)
    qxs = [jnp.concatenate([q_ref[:, _head_cols(hh)], qext], axis=1) for hh in range(hp)]

    carries = _attend_causal(qxs, kx_ref, vx_ref, tk, qi)
    for hh in range(hp):
        o_ref[:, _head_cols(hh)] = _softmax_out(carries[hh], o_ref.dtype)


def _pad_cols(w, n):
    return jnp.pad(w, ((0, 0), (0, n - w.shape[1])))


def kernel(x, p, norm_g, w_in_ab, conv_w, a_log, dt_bias, gdn_norm_g, w_out_ab, w_in_c, forget_b,
           w_out_c, ple_norm_g, w_ple_gate, w_ple_proj, final_g):
    b, s, d = x.shape
    t = b * s
    gdn_heads = a_log.shape[1]
    gdn_w = gdn_heads * HEAD_DIM
    moba_w = (w_out_ab.shape[1] - gdn_w)
    moba_heads = moba_w // HEAD_DIM
    fox_heads = forget_b.shape[1]
    fox_w = fox_heads * HEAD_DIM
    mix_ab = gdn_w + moba_w
    assert s % MOBA_BLOCK == 0 and s % ROW_TILE == 0 and s % GDN_TILE == 0 and s % FOX_TQ == 0
    assert 2 * gdn_heads <= SUBLANES and fox_heads <= LANES
    tm = ROW_TILE
    n_row = t // tm

    xf = x.reshape(t, d)
    pf = p.reshape(p.shape[0], t, PLE_DIM)
    row = lambda v: v.reshape(1, -1).astype(F32)

    w0 = w_in_ab[0]
    c1 = 3 * gdn_w
    c2 = c1 + 2 * gdn_heads
    w0_big = jnp.concatenate([w0[:, :c1], w0[:, c2:c2 + moba_w] * ATT_SCALE, w0[:, c2 + moba_w:]],
                             axis=1).astype(BF16)
    w0_small = w0[:, c1:c2]
    w0_s = _pad_cols(w0_small, LANES).astype(BF16)
    w0_st = jnp.pad(w0_small.T, ((0, SMALL_ROWS - 2 * gdn_heads), (0, 0))).astype(BF16)

    qkv_a, qkv_b, z0, ab, abt = pl.pallas_call(
        _rows_a_kernel,
        grid=(n_row,),
        in_specs=[_row_spec(tm, d), _const_spec((1, d)), _const_spec(w0_big.shape),
                  _const_spec(w0_s.shape), _const_spec(w0_st.shape)],
        out_specs=[_row_spec(tm, c1), _row_spec(tm, 3 * moba_w), _row_spec(tm, mix_ab),
                   _row_spec(tm, LANES),
                   pl.BlockSpec((None, SUBLANES, tm), lambda i: (i // (s // tm), 0, i % (s // tm)))],
        out_shape=[jax.ShapeDtypeStruct((t, c1), BF16), jax.ShapeDtypeStruct((t, 3 * moba_w), BF16),
                   jax.ShapeDtypeStruct((t, mix_ab), BF16), jax.ShapeDtypeStruct((t, LANES), F32),
                   jax.ShapeDtypeStruct((b, SUBLANES, s), F32)],
        compiler_params=_row_params("parallel"),
        name="rows_a",
    )(xf, row(norm_g[0]), w0_big, w0_s, w0_st)

    ts = GDN_TILE
    n_seq = s // ts
    nsq = GDN_SEQS_PER_STEP if b % GDN_SEQS_PER_STEP == 0 else 1
    hp = jnp.zeros((SUBLANES, LANES), F32)
    hp = hp.at[0, :gdn_heads].set(a_log[0]).at[1, :gdn_heads].set(dt_bias[0])
    hpt = jnp.zeros((SUBLANES, LANES), F32)
    hpt = hpt.at[:gdn_heads, 0].set(a_log[0]).at[:gdn_heads, 1].set(dt_bias[0])
    oa = pl.pallas_call(
        functools.partial(_gdn_kernel, heads=gdn_heads),
        grid=(b // nsq, n_seq),
        in_specs=[pl.BlockSpec((nsq, ts, c1), lambda i, j: (i, j, 0)),
                  pl.BlockSpec((nsq, ts, LANES), lambda i, j: (i, j, 0)),
                  pl.BlockSpec((nsq, SUBLANES, ts), lambda i, j: (i, 0, j)),
                  pl.BlockSpec((CONV_K, c1), lambda i, j: (0, 0)),
                  pl.BlockSpec((SUBLANES, LANES), lambda i, j: (0, 0)),
                  pl.BlockSpec((SUBLANES, LANES), lambda i, j: (0, 0)),
                  pl.BlockSpec((1, HEAD_DIM), lambda i, j: (0, 0))],
        out_specs=pl.BlockSpec((nsq, ts, gdn_w), lambda i, j: (i, j, 0)),
        out_shape=jax.ShapeDtypeStruct((b, s, gdn_w), BF16),
        scratch_shapes=[pltpu.VMEM((gdn_heads * nsq, HEAD_DIM, HEAD_DIM), F32),
                        pltpu.VMEM((nsq, SUBLANES + ts, c1), F32)],
        compiler_params=pltpu.CompilerParams(dimension_semantics=("parallel", "arbitrary"),
                                             vmem_limit_bytes=VMEM_LIMIT),
        name="gdn",
    )(qkv_a.reshape(b, s, c1), ab.reshape(b, s, LANES), abt, conv_w[0].astype(F32), hp, hpt,
      row(gdn_norm_g[0]))
    oa = oa.reshape(t, gdn_w)

    nb = s // MOBA_BLOCK
    hpm = min(MOBA_HEADS_PER_STEP, moba_heads)
    assert moba_heads % hpm == 0 and SEL_LANE0 + nb <= LANES and s % MOBA_TK == 0
    slopes = 2.0 ** (-8.0 * (jnp.arange(moba_heads, dtype=F32) + 1.0) / moba_heads)
    qkv_b3 = qkv_b.reshape(b, s, 3 * moba_w)
    mg = moba_heads // hpm
    ob = pl.pallas_call(
        functools.partial(_moba_kernel, nb=nb, hp=hpm, tk=MOBA_TK),
        grid=(b, mg, nb),
        in_specs=[pl.BlockSpec(memory_space=pltpu.SMEM),
                  pl.BlockSpec((None, MOBA_BLOCK, hpm * HEAD_DIM), lambda i, h, j: (i, j, h)),
                  pl.BlockSpec((None, s, hpm * HEAD_DIM), lambda i, h, j: (i, 0, mg + h)),
                  pl.BlockSpec((None, s, hpm * HEAD_DIM), lambda i, h, j: (i, 0, 2 * mg + h))],
        out_specs=pl.BlockSpec((None, MOBA_BLOCK, hpm * HEAD_DIM), lambda i, h, j: (i, j, h)),
        out_shape=jax.ShapeDtypeStruct((b, s, moba_w), BF16),
        scratch_shapes=[pltpu.VMEM((hpm, s, HEAD_DIM + LANES), BF16),
                        pltpu.VMEM((hpm, s, HEAD_DIM + LANES), BF16),
                        pltpu.VMEM((hpm, LANES, HEAD_DIM), F32)],
        compiler_params=pltpu.CompilerParams(
            dimension_semantics=("parallel", "parallel", "arbitrary"), vmem_limit_bytes=VMEM_LIMIT),
        name="moba",
    )(slopes, qkv_b3, qkv_b3, qkv_b3)
    ob = ob.reshape(t, moba_w)

    w1 = w_in_c[0]
    e1 = 3 * fox_w
    e2 = e1 + fox_heads
    w1_big = jnp.concatenate([w1[:, :fox_w] * ATT_SCALE, w1[:, fox_w:e1], w1[:, e2:]],
                             axis=1).astype(BF16)
    w1_f = _pad_cols(w1[:, e1:e2], LANES).astype(BF16)
    fb = _pad_cols(row(forget_b[0]), LANES)
    ple_w = lambda i: (w_ple_gate[i].astype(BF16), w_ple_proj[i].astype(BF16))
    wg0, wp0 = ple_w(0)
    wout0 = w_out_ab[0].astype(BF16)
    x2, qkv_c, z1, cfox = pl.pallas_call(
        functools.partial(_rows_d_kernel, tiles_per_seq=s // tm),
        grid=(n_row,),
        in_specs=[_row_spec(tm, gdn_w), _row_spec(tm, moba_w), _row_spec(tm, mix_ab), _row_spec(tm, d),
                  _layer_spec(tm, PLE_DIM, 0), _const_spec(wout0.shape), _const_spec((1, d)),
                  _const_spec(wg0.shape), _const_spec(wp0.shape), _const_spec((1, d)),
                  _const_spec(w1_big.shape), _const_spec(w1_f.shape), _const_spec(fb.shape)],
        out_specs=[_row_spec(tm, d), _row_spec(tm, e1), _row_spec(tm, fox_w), _row_spec(tm, LANES)],
        out_shape=[jax.ShapeDtypeStruct((t, d), F32), jax.ShapeDtypeStruct((t, e1), BF16),
                   jax.ShapeDtypeStruct((t, fox_w), BF16), jax.ShapeDtypeStruct((t, LANES), F32)],
        scratch_shapes=[pltpu.VMEM((1, LANES), F32)],
        compiler_params=_row_params("arbitrary"),
        name="rows_d",
    )(oa, ob, z0, xf, pf, wout0, row(ple_norm_g[0]), wg0, wp0, row(norm_g[1]), w1_big, w1_f, fb)

    tq, tk = FOX_TQ, FOX_TK
    hpf = min(FOX_HEADS_PER_STEP, fox_heads)
    assert fox_heads % hpf == 0 and tk % tq == 0 and s % tk == 0
    fg = fox_heads // hpf
    qkv_c3 = qkv_c.reshape(b, s, e1)
    oc = pl.pallas_call(
        functools.partial(_fox_kernel, hp=hpf, tk=tk),
        grid=(b, fg, s // tq),
        in_specs=[pl.BlockSpec((None, tq, hpf * HEAD_DIM), lambda i, h, j: (i, j, h)),
                  pl.BlockSpec((None, s, hpf * HEAD_DIM), lambda i, h, j: (i, 0, fg + h)),
                  pl.BlockSpec((None, s, hpf * HEAD_DIM), lambda i, h, j: (i, 0, 2 * fg + h)),
                  pl.BlockSpec((s, LANES), lambda i, h, j: (i, 0))],
        out_specs=pl.BlockSpec((None, tq, hpf * HEAD_DIM), lambda i, h, j: (i, j, h)),
        out_shape=jax.ShapeDtypeStruct((b, s, fox_w), BF16),
        scratch_shapes=[pltpu.VMEM((hpf, s, HEAD_DIM + LANES), BF16),
                        pltpu.VMEM((hpf, s, HEAD_DIM + LANES), BF16)],
        compiler_params=pltpu.CompilerParams(
            dimension_semantics=("parallel", "parallel", "arbitrary"), vmem_limit_bytes=VMEM_LIMIT),
        name="fox",
    )(qkv_c3, qkv_c3, qkv_c3, cfox)
    oc = oc.reshape(t, fox_w)

    wg1, wp1 = ple_w(1)
    wout1 = w_out_c[0].astype(BF16)
    out = pl.pallas_call(
        _rows_f_kernel,
        grid=(n_row,),
        in_specs=[_row_spec(tm, fox_w), _row_spec(tm, fox_w), _row_spec(tm, d), _layer_spec(tm, PLE_DIM, 1),
                  _const_spec(wout1.shape), _const_spec((1, d)), _const_spec(wg1.shape),
                  _const_spec(wp1.shape), _const_spec((1, d))],
        out_specs=_row_spec(tm, d),
        out_shape=jax.ShapeDtypeStruct((t, d), F32),
        compiler_params=_row_params("parallel"),
        name="rows_f",
    )(oc, z1, x2, pf, wout1, row(ple_norm_g[1]), wg1, wp1, row(final_g))
    return out.reshape(b, s, d)
```

```python
import functools

import jax
import jax.numpy as jnp
from jax import lax
from jax.experimental import pallas as pl
from jax.experimental.pallas import tpu as pltpu

F32 = jnp.float32
BF16 = jnp.bfloat16

HEAD_DIM = 128
PLE_DIM = 256
CONV_K = 4
GDN_CHUNK = 64
MOBA_BLOCK = 256
MOBA_TOPK = 3
RMS_EPS = 1e-6
NEG = -(2.0 ** 100)
LANES = 128
SUBLANES = 8
SMALL_ROWS = 16
VMEM_LIMIT = 56 * 1024 * 1024

LOG2E = 1.4426950408889634
ATT_SCALE = HEAD_DIM ** -0.5 * LOG2E

ROW_TILE = 512
PROJ_CHUNK = 512
GDN_TILE = 256
GDN_SEQS_PER_STEP = 2
MOBA_HEADS_PER_STEP = 4
MOBA_TK = 4 * MOBA_BLOCK
FOX_HEADS_PER_STEP = 2
FOX_TQ = 1024
FOX_TK = 1024
BIAS_PARTS = 3
SEL_LANE0 = 8


def _sigmoid(x):
    return 1.0 / (1.0 + jnp.exp(-x))


def _silu(x):
    return x * _sigmoid(x)


def _softplus(x):
    return jnp.maximum(x, 0.0) + jnp.log1p(jnp.exp(-jnp.abs(x)))


def _log_sigmoid(x):
    return jnp.minimum(x, 0.0) - jnp.log1p(jnp.exp(-jnp.abs(x)))


def _rms(xf, g):
    return xf * lax.rsqrt(jnp.mean(xf * xf, axis=-1, keepdims=True) + RMS_EPS) * g


def _dot(a, b):
    return jnp.dot(a, b, preferred_element_type=F32)


def _dot_nt(a, b):
    return lax.dot_general(a, b, (((1,), (1,)), ((), ())), preferred_element_type=F32)


def _seg_cumsum(v, seg, axis):
    pos = lax.broadcasted_iota(jnp.int32, v.shape, axis) & (seg - 1)
    s = 1
    while s < seg:
        v = v + jnp.where(pos >= s, pltpu.roll(v, s, axis=axis), 0.0)
        s *= 2
    return v


def _mix_ple(o_parts, z, x, p, w_out_ref, gple, w_gate_ref, w_proj_ref):
    gz = _silu(z.astype(F32))
    acc = x
    off = 0
    for o in o_parts:
        w = o.shape[1]
        y = (o.astype(F32) * gz[:, off:off + w]).astype(BF16)
        acc = acc + _dot(y, w_out_ref[off:off + w, :])
        off += w
    hn = _rms(acc, gple).astype(BF16)
    gate = _sigmoid(_dot(hn, w_gate_ref[...]))
    pp = _dot(p.astype(BF16), w_proj_ref[...])
    return acc + gate * pp


def _project(hn, w_ref, out_refs):
    off = 0
    for o_ref in out_refs:
        n = o_ref.shape[1]
        for c0 in range(0, n, PROJ_CHUNK):
            c1 = min(n, c0 + PROJ_CHUNK)
            o_ref[:, c0:c1] = _dot(hn, w_ref[:, off + c0:off + c1]).astype(o_ref.dtype)
        off += n


def _rows_a_kernel(x_ref, g_ref, w_ref, ws_ref, wst_ref,
                   qkva_ref, qkvb_ref, z_ref, ab_ref, abt_ref):
    hn = _rms(x_ref[...], g_ref[...]).astype(BF16)
    _project(hn, w_ref, (qkva_ref, qkvb_ref, z_ref))
    ab_ref[...] = _dot(hn, ws_ref[...])
    abt_ref[...] = _dot_nt(wst_ref[...], hn)[:SUBLANES]


def _rows_d_kernel(oa_ref, ob_ref, z_ref, x_ref, p_ref, wout_ref, gple_ref, wgate_ref, wproj_ref,
                   g_ref, w_ref, wf_ref, fb_ref,
                   x2_ref, qkv_ref, z1_ref, c_ref, carry_ref, *, tiles_per_seq):
    x2 = _mix_ple((oa_ref[...], ob_ref[...]), z_ref[...], x_ref[...], p_ref[...],
                  wout_ref, gple_ref[...], wgate_ref, wproj_ref)
    x2_ref[...] = x2
    hn = _rms(x2, g_ref[...]).astype(BF16)
    _project(hn, w_ref, (qkv_ref, z1_ref))

    @pl.when(pl.program_id(0) % tiles_per_seq == 0)
    def _():
        carry_ref[...] = jnp.zeros_like(carry_ref)

    log_f = _log_sigmoid(_dot(hn, wf_ref[...]) + fb_ref[...])
    tm = log_f.shape[0]
    c = _seg_cumsum(log_f, tm, 0) + carry_ref[...]
    c_ref[...] = c
    carry_ref[...] = c[tm - 1:tm, :]


def _rows_f_kernel(o_ref, z_ref, x_ref, p_ref, wout_ref, gple_ref, wgate_ref, wproj_ref,
                   g_ref, out_ref):
    x2 = _mix_ple((o_ref[...],), z_ref[...], x_ref[...], p_ref[...],
                  wout_ref, gple_ref[...], wgate_ref, wproj_ref)
    out_ref[...] = _rms(x2, g_ref[...])


def _row_spec(tm, n):
    return pl.BlockSpec((tm, n), lambda i: (i, 0))


def _layer_spec(tm, n, layer):
    return pl.BlockSpec((None, tm, n), lambda i: (layer, i, 0))


def _const_spec(shape):
    return pl.BlockSpec(shape, lambda i: (0,) * len(shape), pipeline_mode=pl.Buffered(1))


def _row_params(semantics):
    return pltpu.CompilerParams(dimension_semantics=(semantics,), vmem_limit_bytes=VMEM_LIMIT)


def _bdot(a, b):
    return lax.dot_general(a, b, (((2,), (1,)), ((0,), (0,))), preferred_element_type=F32)


def _bdot_nt(a, b):
    return lax.dot_general(a, b, (((2,), (2,)), ((0,), (0,))), preferred_element_type=F32)


def _bdot_tn(a, b):
    return lax.dot_general(a, b, (((1,), (1,)), ((0,), (0,))), preferred_element_type=F32)


def _inv_unit_lower(a):
    c = a.shape[-1]
    eye = (lax.broadcasted_iota(jnp.int32, (c, c), 0)
           == lax.broadcasted_iota(jnp.int32, (c, c), 1)).astype(F32)
    t = eye - a
    p = a
    k = 2
    while k < c:
        pb = p.astype(BF16)
        p = _bdot(pb, pb)
        t = t + _bdot(t.astype(BF16), p.astype(BF16))
        k *= 2
    return t


def _gdn_kernel(qkv_ref, ab_ref, abt_ref, cw_ref, shift_ref, hp_ref, hpt_ref, gn_ref,
                o_ref, state_ref, tail_ref, *, heads):
    nseq, ts = qkv_ref.shape[0], qkv_ref.shape[1]
    width = heads * HEAD_DIM
    ck = GDN_CHUNK
    nc = ts // ck

    @pl.when(pl.program_id(1) == 0)
    def _():
        state_ref[...] = jnp.zeros_like(state_ref)
        tail_ref[...] = jnp.zeros_like(tail_ref)

    cw = cw_ref[...]
    hp = hp_ref[...]
    hpt = hpt_ref[...]
    xcs, beta_cols, gcum_cols, gcum_rows = [], [], [], []
    for sq in range(nseq):
        ub = qkv_ref[sq]
        u = ub.astype(F32)
        prev = tail_ref[sq]
        tail_ref[sq] = u[ts - SUBLANES:, :]
        row8 = lax.broadcasted_iota(jnp.int32, prev.shape, 0)
        acc = u * cw[CONV_K - 1:CONV_K, :]
        head = jnp.zeros(prev.shape, F32)
        for s in range(1, CONV_K):
            w_s = cw[CONV_K - 1 - s:CONV_K - s, :]
            acc = acc + _dot(shift_ref[s - 1], ub) * w_s
            head = head + jnp.where(row8 < s, pltpu.roll(prev, s, axis=0), 0.0) * w_s
        acc = jnp.concatenate([acc[:SUBLANES] + head, acc[SUBLANES:]], axis=0)
        xcs.append(_silu(acc))
        ab = ab_ref[sq]
        g_col = -jnp.exp(hp[0:1, :]) * _softplus(ab + hp[1:2, :])
        beta_cols.append(_sigmoid(ab))
        gcum_cols.append(_seg_cumsum(g_col, ck, 0))
        g_row = -jnp.exp(hpt[:, 0:1]) * _softplus(abt_ref[sq] + hpt[:, 1:2])
        gcum_rows.append(_seg_cumsum(g_row, ck, 1))

    groups = [(h, sq) for h in range(heads) for sq in range(nseq)]
    ng = len(groups)

    def chunks(col0):
        return jnp.concatenate(
            [xcs[sq][:, col0 + h * HEAD_DIM:col0 + (h + 1) * HEAD_DIM].reshape(nc, ck, HEAD_DIM)
             for h, sq in groups], axis=0)

    def col_chunks(cols, lane0):
        return jnp.concatenate(
            [cols[sq][:, lane0 + h:lane0 + h + 1].reshape(nc, ck, 1) for h, sq in groups], axis=0)

    q3, k3, v3 = chunks(0), chunks(width), chunks(2 * width)
    q3 = q3 * lax.rsqrt(jnp.sum(q3 * q3, axis=-1, keepdims=True) + RMS_EPS) * (HEAD_DIM ** -0.5)
    k3 = k3 * lax.rsqrt(jnp.sum(k3 * k3, axis=-1, keepdims=True) + RMS_EPS)
    beta3 = col_chunks(beta_cols, heads)
    gc3 = col_chunks(gcum_cols, 0)
    gr3 = jnp.stack([gcum_rows[sq][h:h + 1, c * ck:(c + 1) * ck]
                     for h, sq in groups for c in range(nc)], axis=0)

    ri = lax.broadcasted_iota(jnp.int32, (ck, ck), 0)
    ci = lax.broadcasted_iota(jnp.int32, (ck, ck), 1)
    incl = ri >= ci
    strict = ri > ci
    decay = jnp.where(incl, jnp.exp(jnp.where(incl, gc3 - gr3, 0.0)), 0.0)
    kb3 = k3 * beta3
    k3b = k3.astype(BF16)
    a_mat = jnp.where(strict, _bdot_nt(kb3.astype(BF16), k3b) * decay, 0.0)
    t_inv = _inv_unit_lower(a_mat)
    exg = jnp.exp(gc3)
    rhs = jnp.concatenate([v3 * beta3, kb3 * exg], axis=2).astype(BF16)
    sol = _bdot(t_inv.astype(BF16), rhs)
    u3, w3 = sol[:, :, :HEAD_DIM], sol[:, :, HEAD_DIM:]
    qk3 = jnp.where(incl, _bdot_nt(q3.astype(BF16), k3b) * decay, 0.0).astype(BF16)
    g_last = gc3[:, ck - 1:ck, :]
    wq3 = jnp.concatenate([w3, q3 * exg], axis=1).astype(BF16)
    kdec3 = (k3 * jnp.exp(g_last - gc3)).astype(BF16)
    gl3 = jnp.exp(g_last)

    by_group = lambda a: a.reshape((ng, nc) + a.shape[1:])
    u4, qk4, wq4, kdec4, gl4 = (by_group(a) for a in (u3, qk3, wq3, kdec3, gl3))
    gn = gn_ref[...]
    state = state_ref[...]
    for c in range(nc):
        ws = _bdot(wq4[:, c], state.astype(BF16))
        v_new = (u4[:, c] - ws[:, :ck]).astype(BF16)
        o = ws[:, ck:] + _bdot(qk4[:, c], v_new)
        state = state * gl4[:, c] + _bdot_tn(kdec4[:, c], v_new)
        o = _rms(o, gn).astype(o_ref.dtype)
        for gi, (h, sq) in enumerate(groups):
            o_ref[sq, c * ck:(c + 1) * ck, h * HEAD_DIM:(h + 1) * HEAD_DIM] = o[gi]
    state_ref[...] = state


def _split_bias(x):
    parts = []
    for _ in range(BIAS_PARTS):
        piece = x.astype(BF16).astype(F32)
        parts.append(piece)
        x = x - piece
    return parts


def _key_ext(bias, extra=None):
    lane = lax.broadcasted_iota(jnp.int32, (bias.shape[0], LANES), 1)
    ext = jnp.zeros((bias.shape[0], LANES), F32) if extra is None else extra
    for i, piece in enumerate(_split_bias(bias)):
        ext = jnp.where(lane == i, piece, ext)
    return ext.astype(BF16)


def _ones_lane0(n):
    lane = lax.broadcasted_iota(jnp.int32, (n, LANES), 1)
    return jnp.where(lane == 0, 1.0, 0.0).astype(BF16)


def _attend_init(qxs):
    tq = qxs[0].shape[0]
    return tuple((jnp.full((tq, 1), NEG, F32), jnp.zeros((tq, HEAD_DIM + LANES), F32)) for _ in qxs)


def _attend_tile(qxs, kx_ref, vx_ref, tk, k0, mask, carries):
    heads = range(len(qxs))
    ss = [_dot_nt(qxs[h], kx_ref[h, pl.ds(k0, tk), :]) for h in heads]
    if mask is not None:
        ss = [jnp.where(mask, s, NEG) for s in ss]
    ms = [jnp.maximum(m, jnp.max(s, axis=1, keepdims=True)) for s, (m, _) in zip(ss, carries)]
    out = []
    for h in heads:
        m, accx = carries[h]
        pv = _dot(jnp.exp2(ss[h] - ms[h]).astype(BF16), vx_ref[h, pl.ds(k0, tk), :])
        out.append((ms[h], jnp.exp2(m - ms[h]) * accx + pv))
    return tuple(out)


def _attend_causal(qxs, kx_ref, vx_ref, tk, qi):
    tq = qxs[0].shape[0]
    n_past = (qi * tq) // tk

    def tile(t, mask, carries):
        return _attend_tile(qxs, kx_ref, vx_ref, tk, pl.multiple_of(t * tk, tk), mask, carries)

    carries = lax.fori_loop(0, n_past, lambda t, cr: tile(t, None, cr), _attend_init(qxs))

    def causal(rows, width, row0):
        return (lax.broadcasted_iota(jnp.int32, (rows, width), 0) + row0
                >= lax.broadcasted_iota(jnp.int32, (rows, width), 1))

    if tq != tk:
        return tile(n_past, causal(tq, tk, qi * tq - n_past * tk), carries)
    half = tq // 2
    k0 = pl.multiple_of(qi * tq, tq)
    top = _attend_tile([qx[:half] for qx in qxs], kx_ref, vx_ref, half, k0, causal(half, half, 0),
                       tuple((m[:half], accx[:half]) for m, accx in carries))
    bot = _attend_tile([qx[half:] for qx in qxs], kx_ref, vx_ref, tk, k0, causal(half, tk, half),
                       tuple((m[half:], accx[half:]) for m, accx in carries))
    return tuple((jnp.concatenate([mt, mb], axis=0), jnp.concatenate([at, ab], axis=0))
                 for (mt, at), (mb, ab) in zip(top, bot))


def _softmax_out(carry, dtype):
    _, accx = carry
    return (accx[:, :HEAD_DIM] / accx[:, HEAD_DIM:HEAD_DIM + 1]).astype(dtype)


def _head_cols(hh):
    return slice(hh * HEAD_DIM, (hh + 1) * HEAD_DIM)


def _moba_kernel(slopes_ref, q_ref, k_ref, v_ref, o_ref, kx_ref, vx_ref, kmean_ref, *, nb, hp, tk):
    blk = MOBA_BLOCK
    s_len = k_ref.shape[0]
    hg = pl.program_id(1)
    qi = pl.program_id(2)
    nbp = -(-nb // SUBLANES) * SUBLANES

    @pl.when(qi == 0)
    def _():
        pos = lax.broadcasted_iota(jnp.int32, (s_len, 1), 0)
        lane = lax.broadcasted_iota(jnp.int32, (s_len, LANES), 1)
        onehot = jnp.where(lane - SEL_LANE0 == lax.broadcasted_iota(jnp.int32, (s_len, LANES), 0) // blk,
                           1.0, 0.0)
        ones = _ones_lane0(s_len)
        kmean_ref[...] = jnp.zeros_like(kmean_ref)
        for hh in range(hp):
            slope2 = slopes_ref[hg * hp + hh] * LOG2E
            kx_ref[hh, :, :HEAD_DIM] = k_ref[:, _head_cols(hh)]
            kx_ref[hh, :, HEAD_DIM:] = _key_ext(slope2 * pos.astype(F32), onehot)
            vx_ref[hh, :, :HEAD_DIM] = v_ref[:, _head_cols(hh)]
            vx_ref[hh, :, HEAD_DIM:] = ones
            for j in range(nb):
                kmean_ref[hh, j:j + 1, :] = jnp.mean(
                    k_ref[j * blk:(j + 1) * blk, _head_cols(hh)].astype(F32), axis=0, keepdims=True)

    blk_id = lax.broadcasted_iota(jnp.int32, (nbp, blk), 0)
    past = blk_id < qi
    lane = lax.broadcasted_iota(jnp.int32, (blk, LANES), 1)
    qxs = []
    for hh in range(hp):
        q = q_ref[:, _head_cols(hh)]
        gate = sum(_dot_nt(piece.astype(BF16), q) for piece in _split_bias(kmean_ref[hh, :nbp, :]))
        gm = jnp.where(past, gate, NEG)
        rank = jnp.zeros(gm.shape, jnp.int32)
        for j in range(nb):
            cj = gm[j:j + 1, :]
            beats = (cj > gm) | ((cj == gm) & (blk_id > j))
            rank = rank + beats.astype(jnp.int32)
        sel_t = jnp.where(((rank < MOBA_TOPK) & past) | (blk_id == qi), 0.0, NEG)
        sel_t = jnp.concatenate([jnp.zeros((SEL_LANE0, blk), F32), sel_t,
                                 jnp.zeros((LANES - SEL_LANE0 - nbp, blk), F32)], axis=0)
        qext = jnp.where(lane < BIAS_PARTS, 1.0, sel_t.T).astype(BF16)
        qxs.append(jnp.concatenate([q, qext], axis=1))

    carries = _attend_causal(qxs, kx_ref, vx_ref, tk, qi)
    for hh in range(hp):
        o_ref[:, _head_cols(hh)] = _softmax_out(carries[hh], o_ref.dtype)


def _fox_kernel(q_ref, k_ref, v_ref, c_ref, o_ref, kx_ref, vx_ref, *, hp, tk):
    tq = q_ref.shape[0]
    s_len = k_ref.shape[0]
    hg = pl.program_id(1)
    qi = pl.program_id(2)

    @pl.when(qi == 0)
    def _():
        lane = lax.broadcasted_iota(jnp.int32, (s_len, LANES), 1)
        ones = _ones_lane0(s_len)
        for hh in range(hp):
            c_col = jnp.sum(jnp.where(lane == hg * hp + hh, c_ref[...], 0.0), axis=1, keepdims=True)
            kx_ref[hh, :, :HEAD_DIM] = k_ref[:, _head_cols(hh)]
            kx_ref[hh, :, HEAD_DIM:] = _key_ext(-LOG2E * c_col)
            vx_ref[hh, :, :HEAD_DIM] = v_ref[:, _head_cols(hh)]
            vx_ref[hh, :, HEAD_DIM:] = ones

    lane = lax.broadcasted_iota(jnp.int32, (tq, LANES), 1)
    qext = jnp.where(lane < BIAS_PARTS, 1.0, 0.0).astype(BF16)
    qxs = [jnp.concatenate([q_ref[:, _head_cols(hh)], qext], axis=1) for hh in range(hp)]

    carries = _attend_causal(qxs, kx_ref, vx_ref, tk, qi)
    for hh in range(hp):
        o_ref[:, _head_cols(hh)] = _softmax_out(carries[hh], o_ref.dtype)


def _pad_cols(w, n):
    return jnp.pad(w, ((0, 0), (0, n - w.shape[1])))


def kernel(x, p, norm_g, w_in_ab, conv_w, a_log, dt_bias, gdn_norm_g, w_out_ab, w_in_c, forget_b,
           w_out_c, ple_norm_g, w_ple_gate, w_ple_proj, final_g):
    b, s, d = x.shape
    t = b * s
    gdn_heads = a_log.shape[1]
    gdn_w = gdn_heads * HEAD_DIM
    moba_w = (w_out_ab.shape[1] - gdn_w)
    moba_heads = moba_w // HEAD_DIM
    fox_heads = forget_b.shape[1]
    fox_w = fox_heads * HEAD_DIM
    mix_ab = gdn_w + moba_w
    assert s % MOBA_BLOCK == 0 and s % ROW_TILE == 0 and s % GDN_TILE == 0 and s % FOX_TQ == 0
    assert 2 * gdn_heads <= SUBLANES and fox_heads <= LANES
    tm = ROW_TILE
    n_row = t // tm

    xf = x.reshape(t, d)
    pf = p.reshape(p.shape[0], t, PLE_DIM)
    row = lambda v: v.reshape(1, -1).astype(F32)

    w0 = w_in_ab[0]
    c1 = 3 * gdn_w
    c2 = c1 + 2 * gdn_heads
    w0_big = jnp.concatenate([w0[:, :c1], w0[:, c2:c2 + moba_w] * ATT_SCALE, w0[:, c2 + moba_w:]],
                             axis=1).astype(BF16)
    w0_small = w0[:, c1:c2]
    w0_s = _pad_cols(w0_small, LANES).astype(BF16)
    w0_st = jnp.pad(w0_small.T, ((0, SMALL_ROWS - 2 * gdn_heads), (0, 0))).astype(BF16)

    qkv_a, qkv_b, z0, ab, abt = pl.pallas_call(
        _rows_a_kernel,
        grid=(n_row,),
        in_specs=[_row_spec(tm, d), _const_spec((1, d)), _const_spec(w0_big.shape),
                  _const_spec(w0_s.shape), _const_spec(w0_st.shape)],
        out_specs=[_row_spec(tm, c1), _row_spec(tm, 3 * moba_w), _row_spec(tm, mix_ab),
                   _row_spec(tm, LANES),
                   pl.BlockSpec((None, SUBLANES, tm), lambda i: (i // (s // tm), 0, i % (s // tm)))],
        out_shape=[jax.ShapeDtypeStruct((t, c1), BF16), jax.ShapeDtypeStruct((t, 3 * moba_w), BF16),
                   jax.ShapeDtypeStruct((t, mix_ab), BF16), jax.ShapeDtypeStruct((t, LANES), F32),
                   jax.ShapeDtypeStruct((b, SUBLANES, s), F32)],
        compiler_params=_row_params("parallel"),
        name="rows_a",
    )(xf, row(norm_g[0]), w0_big, w0_s, w0_st)

    ts = GDN_TILE
    n_seq = s // ts
    nsq = GDN_SEQS_PER_STEP if b % GDN_SEQS_PER_STEP == 0 else 1
    tpos = jnp.arange(ts)
    shifts = jnp.stack([(tpos[:, None] - tpos[None, :] == sh).astype(BF16) for sh in range(1, CONV_K)])
    hp = jnp.zeros((SUBLANES, LANES), F32)
    hp = hp.at[0, :gdn_heads].set(a_log[0]).at[1, :gdn_heads].set(dt_bias[0])
    hpt = jnp.zeros((SUBLANES, LANES), F32)
    hpt = hpt.at[:gdn_heads, 0].set(a_log[0]).at[:gdn_heads, 1].set(dt_bias[0])
    oa = pl.pallas_call(
        functools.partial(_gdn_kernel, heads=gdn_heads),
        grid=(b // nsq, n_seq),
        in_specs=[pl.BlockSpec((nsq, ts, c1), lambda i, j: (i, j, 0)),
                  pl.BlockSpec((nsq, ts, LANES), lambda i, j: (i, j, 0)),
                  pl.BlockSpec((nsq, SUBLANES, ts), lambda i, j: (i, 0, j)),
                  pl.BlockSpec((CONV_K, c1), lambda i, j: (0, 0)),
                  pl.BlockSpec((CONV_K - 1, ts, ts), lambda i, j: (0, 0, 0)),
                  pl.BlockSpec((SUBLANES, LANES), lambda i, j: (0, 0)),
                  pl.BlockSpec((SUBLANES, LANES), lambda i, j: (0, 0)),
                  pl.BlockSpec((1, HEAD_DIM), lambda i, j: (0, 0))],
        out_specs=pl.BlockSpec((nsq, ts, gdn_w), lambda i, j: (i, j, 0)),
        out_shape=jax.ShapeDtypeStruct((b, s, gdn_w), BF16),
        scratch_shapes=[pltpu.VMEM((gdn_heads * nsq, HEAD_DIM, HEAD_DIM), F32),
                        pltpu.VMEM((nsq, SUBLANES, c1), F32)],
        compiler_params=pltpu.CompilerParams(dimension_semantics=("parallel", "arbitrary"),
                                             vmem_limit_bytes=VMEM_LIMIT),
        name="gdn",
    )(qkv_a.reshape(b, s, c1), ab.reshape(b, s, LANES), abt, conv_w[0].astype(F32), shifts, hp, hpt,
      row(gdn_norm_g[0]))
    oa = oa.reshape(t, gdn_w)

    nb = s // MOBA_BLOCK
    hpm = min(MOBA_HEADS_PER_STEP, moba_heads)
    assert moba_heads % hpm == 0 and SEL_LANE0 + nb <= LANES and s % MOBA_TK == 0
    slopes = 2.0 ** (-8.0 * (jnp.arange(moba_heads, dtype=F32) + 1.0) / moba_heads)
    qkv_b3 = qkv_b.reshape(b, s, 3 * moba_w)
    mg = moba_heads // hpm
    ob = pl.pallas_call(
        functools.partial(_moba_kernel, nb=nb, hp=hpm, tk=MOBA_TK),
        grid=(b, mg, nb),
        in_specs=[pl.BlockSpec(memory_space=pltpu.SMEM),
                  pl.BlockSpec((None, MOBA_BLOCK, hpm * HEAD_DIM), lambda i, h, j: (i, j, h)),
                  pl.BlockSpec((None, s, hpm * HEAD_DIM), lambda i, h, j: (i, 0, mg + h)),
                  pl.BlockSpec((None, s, hpm * HEAD_DIM), lambda i, h, j: (i, 0, 2 * mg + h))],
        out_specs=pl.BlockSpec((None, MOBA_BLOCK, hpm * HEAD_DIM), lambda i, h, j: (i, j, h)),
        out_shape=jax.ShapeDtypeStruct((b, s, moba_w), BF16),
        scratch_shapes=[pltpu.VMEM((hpm, s, HEAD_DIM + LANES), BF16),
                        pltpu.VMEM((hpm, s, HEAD_DIM + LANES), BF16),
                        pltpu.VMEM((hpm, LANES, HEAD_DIM), F32)],
        compiler_params=pltpu.CompilerParams(
            dimension_semantics=("parallel", "parallel", "arbitrary"), vmem_limit_bytes=VMEM_LIMIT),
        name="moba",
    )(slopes, qkv_b3, qkv_b3, qkv_b3)
    ob = ob.reshape(t, moba_w)

    w1 = w_in_c[0]
    e1 = 3 * fox_w
    e2 = e1 + fox_heads
    w1_big = jnp.concatenate([w1[:, :fox_w] * ATT_SCALE, w1[:, fox_w:e1], w1[:, e2:]],
                             axis=1).astype(BF16)
    w1_f = _pad_cols(w1[:, e1:e2], LANES).astype(BF16)
    fb = _pad_cols(row(forget_b[0]), LANES)
    ple_w = lambda i: (w_ple_gate[i].astype(BF16), w_ple_proj[i].astype(BF16))
    wg0, wp0 = ple_w(0)
    wout0 = w_out_ab[0].astype(BF16)
    x2, qkv_c, z1, cfox = pl.pallas_call(
        functools.partial(_rows_d_kernel, tiles_per_seq=s // tm),
        grid=(n_row,),
        in_specs=[_row_spec(tm, gdn_w), _row_spec(tm, moba_w), _row_spec(tm, mix_ab), _row_spec(tm, d),
                  _layer_spec(tm, PLE_DIM, 0), _const_spec(wout0.shape), _const_spec((1, d)),
                  _const_spec(wg0.shape), _const_spec(wp0.shape), _const_spec((1, d)),
                  _const_spec(w1_big.shape), _const_spec(w1_f.shape), _const_spec(fb.shape)],
        out_specs=[_row_spec(tm, d), _row_spec(tm, e1), _row_spec(tm, fox_w), _row_spec(tm, LANES)],
        out_shape=[jax.ShapeDtypeStruct((t, d), F32), jax.ShapeDtypeStruct((t, e1), BF16),
                   jax.ShapeDtypeStruct((t, fox_w), BF16), jax.ShapeDtypeStruct((t, LANES), F32)],
        scratch_shapes=[pltpu.VMEM((1, LANES), F32)],
        compiler_params=_row_params("arbitrary"),
        name="rows_d",
    )(oa, ob, z0, xf, pf, wout0, row(ple_norm_g[0]), wg0, wp0, row(norm_g[1]), w1_big, w1_f, fb)

    tq, tk = FOX_TQ, FOX_TK
    hpf = min(FOX_HEADS_PER_STEP, fox_heads)
    assert fox_heads % hpf == 0 and tk % tq == 0 and s % tk == 0
    fg = fox_heads // hpf
    qkv_c3 = qkv_c.reshape(b, s, e1)
    oc = pl.pallas_call(
        functools.partial(_fox_kernel, hp=hpf, tk=tk),
        grid=(b, fg, s // tq),
        in_specs=[pl.BlockSpec((None, tq, hpf * HEAD_DIM), lambda i, h, j: (i, j, h)),
                  pl.BlockSpec((None, s, hpf * HEAD_DIM), lambda i, h, j: (i, 0, fg + h)),
                  pl.BlockSpec((None, s, hpf * HEAD_DIM), lambda i, h, j: (i, 0, 2 * fg + h)),
                  pl.BlockSpec((s, LANES), lambda i, h, j: (i, 0))],
        out_specs=pl.BlockSpec((None, tq, hpf * HEAD_DIM), lambda i, h, j: (i, j, h)),
        out_shape=jax.ShapeDtypeStruct((b, s, fox_w), BF16),
        scratch_shapes=[pltpu.VMEM((hpf, s, HEAD_DIM + LANES), BF16),
                        pltpu.VMEM((hpf, s, HEAD_DIM + LANES), BF16)],
        compiler_params=pltpu.CompilerParams(
            dimension_semantics=("parallel", "parallel", "arbitrary"), vmem_limit_bytes=VMEM_LIMIT),
        name="fox",
    )(qkv_c3, qkv_c3, qkv_c3, cfox)
    oc = oc.reshape(t, fox_w)

    wg1, wp1 = ple_w(1)
    wout1 = w_out_c[0].astype(BF16)
    out = pl.pallas_call(
        _rows_f_kernel,
        grid=(n_row,),
        in_specs=[_row_spec(tm, fox_w), _row_spec(tm, fox_w), _row_spec(tm, d), _layer_spec(tm, PLE_DIM, 1),
                  _const_spec(wout1.shape), _const_spec((1, d)), _const_spec(wg1.shape),
                  _const_spec(wp1.shape), _const_spec((1, d))],
        out_specs=_row_spec(tm, d),
        out_shape=jax.ShapeDtypeStruct((t, d), F32),
        compiler_params=_row_params("parallel"),
        name="rows_f",
    )(oc, z1, x2, pf, wout1, row(ple_norm_g[1]), wg1, wp1, row(final_g))
    return out.reshape(b, s, d)
```

```python
import functools

import jax
import jax.numpy as jnp
from jax import lax
from jax.experimental import pallas as pl
from jax.experimental.pallas import tpu as pltpu

F32 = jnp.float32
BF16 = jnp.bfloat16

HEAD_DIM = 128
PLE_DIM = 256
CONV_K = 4
GDN_CHUNK = 64
MOBA_BLOCK = 256
MOBA_TOPK = 3
RMS_EPS = 1e-6
NEG = -(2.0 ** 100)
LANES = 128
SUBLANES = 8
SMALL_ROWS = 16
VMEM_LIMIT = 56 * 1024 * 1024

LOG2E = 1.4426950408889634
ATT_SCALE = HEAD_DIM ** -0.5 * LOG2E

ROW_TILE = 512
PROJ_CHUNK = 512
GDN_TILE = 256
GDN_SEQS_PER_STEP = 4
MOBA_HEADS_PER_STEP = 4
MOBA_TK = 4 * MOBA_BLOCK
FOX_HEADS_PER_STEP = 2
FOX_TQ = 1024
FOX_TK = 1024
BIAS_PARTS = 3
SEL_LANE0 = 8


def _sigmoid(x):
    return 1.0 / (1.0 + jnp.exp(-x))


def _silu(x):
    return x * _sigmoid(x)


def _softplus(x):
    return jnp.maximum(x, 0.0) + jnp.log1p(jnp.exp(-jnp.abs(x)))


def _log_sigmoid(x):
    return jnp.minimum(x, 0.0) - jnp.log1p(jnp.exp(-jnp.abs(x)))


def _rms(xf, g):
    return xf * lax.rsqrt(jnp.mean(xf * xf, axis=-1, keepdims=True) + RMS_EPS) * g


def _dot(a, b):
    return jnp.dot(a, b, preferred_element_type=F32)


def _dot_nt(a, b):
    return lax.dot_general(a, b, (((1,), (1,)), ((), ())), preferred_element_type=F32)


def _seg_cumsum(v, seg, axis):
    pos = lax.broadcasted_iota(jnp.int32, v.shape, axis) & (seg - 1)
    s = 1
    while s < seg:
        v = v + jnp.where(pos >= s, pltpu.roll(v, s, axis=axis), 0.0)
        s *= 2
    return v


def _mix_ple(o_parts, z, x, p, w_out_ref, gple, w_gate_ref, w_proj_ref):
    gz = _silu(z.astype(F32))
    acc = x
    off = 0
    for o in o_parts:
        w = o.shape[1]
        y = (o.astype(F32) * gz[:, off:off + w]).astype(BF16)
        acc = acc + _dot(y, w_out_ref[off:off + w, :])
        off += w
    hn = _rms(acc, gple).astype(BF16)
    gate = _sigmoid(_dot(hn, w_gate_ref[...]))
    pp = _dot(p.astype(BF16), w_proj_ref[...])
    return acc + gate * pp


def _project(hn, w_ref, out_refs):
    off = 0
    for o_ref in out_refs:
        n = o_ref.shape[1]
        for c0 in range(0, n, PROJ_CHUNK):
            c1 = min(n, c0 + PROJ_CHUNK)
            o_ref[:, c0:c1] = _dot(hn, w_ref[:, off + c0:off + c1]).astype(o_ref.dtype)
        off += n


def _rows_a_kernel(x_ref, g_ref, w_ref, ws_ref, wst_ref,
                   qkva_ref, qkvb_ref, z_ref, ab_ref, abt_ref):
    hn = _rms(x_ref[...], g_ref[...]).astype(BF16)
    _project(hn, w_ref, (qkva_ref, qkvb_ref, z_ref))
    ab_ref[...] = _dot(hn, ws_ref[...])
    abt_ref[...] = _dot_nt(wst_ref[...], hn)[:SUBLANES]


def _rows_d_kernel(oa_ref, ob_ref, z_ref, x_ref, p_ref, wout_ref, gple_ref, wgate_ref, wproj_ref,
                   g_ref, w_ref, wf_ref, fb_ref,
                   x2_ref, qkv_ref, z1_ref, c_ref, carry_ref, *, tiles_per_seq):
    x2 = _mix_ple((oa_ref[...], ob_ref[...]), z_ref[...], x_ref[...], p_ref[...],
                  wout_ref, gple_ref[...], wgate_ref, wproj_ref)
    x2_ref[...] = x2
    hn = _rms(x2, g_ref[...]).astype(BF16)
    _project(hn, w_ref, (qkv_ref, z1_ref))

    @pl.when(pl.program_id(0) % tiles_per_seq == 0)
    def _():
        carry_ref[...] = jnp.zeros_like(carry_ref)

    log_f = _log_sigmoid(_dot(hn, wf_ref[...]) + fb_ref[...])
    tm = log_f.shape[0]
    c = _seg_cumsum(log_f, tm, 0) + carry_ref[...]
    c_ref[...] = c
    carry_ref[...] = c[tm - 1:tm, :]


def _rows_f_kernel(o_ref, z_ref, x_ref, p_ref, wout_ref, gple_ref, wgate_ref, wproj_ref,
                   g_ref, out_ref):
    x2 = _mix_ple((o_ref[...],), z_ref[...], x_ref[...], p_ref[...],
                  wout_ref, gple_ref[...], wgate_ref, wproj_ref)
    out_ref[...] = _rms(x2, g_ref[...])


def _row_spec(tm, n):
    return pl.BlockSpec((tm, n), lambda i: (i, 0))


def _layer_spec(tm, n, layer):
    return pl.BlockSpec((None, tm, n), lambda i: (layer, i, 0))


def _const_spec(shape):
    return pl.BlockSpec(shape, lambda i: (0,) * len(shape), pipeline_mode=pl.Buffered(1))


def _row_params(semantics):
    return pltpu.CompilerParams(dimension_semantics=(semantics,), vmem_limit_bytes=VMEM_LIMIT)


def _bdot(a, b):
    return lax.dot_general(a, b, (((2,), (1,)), ((0,), (0,))), preferred_element_type=F32)


def _bdot_nt(a, b):
    return lax.dot_general(a, b, (((2,), (2,)), ((0,), (0,))), preferred_element_type=F32)


def _bdot_tn(a, b):
    return lax.dot_general(a, b, (((1,), (1,)), ((0,), (0,))), preferred_element_type=F32)


def _inv_unit_lower(a):
    c = a.shape[-1]
    eye = (lax.broadcasted_iota(jnp.int32, (c, c), 0)
           == lax.broadcasted_iota(jnp.int32, (c, c), 1)).astype(F32)
    t = eye - a
    p = a
    k = 2
    while k < c:
        pb = p.astype(BF16)
        p = _bdot(pb, pb)
        t = t + _bdot(t.astype(BF16), p.astype(BF16))
        k *= 2
    return t


def _gdn_kernel(qkv_ref, ab_ref, abt_ref, cw_ref, shift_ref, hp_ref, hpt_ref, gn_ref,
                o_ref, state_ref, tail_ref, *, heads):
    nseq, ts = qkv_ref.shape[0], qkv_ref.shape[1]
    width = heads * HEAD_DIM
    ck = GDN_CHUNK
    nc = ts // ck

    @pl.when(pl.program_id(1) == 0)
    def _():
        state_ref[...] = jnp.zeros_like(state_ref)
        tail_ref[...] = jnp.zeros_like(tail_ref)

    cw = cw_ref[...]
    hp = hp_ref[...]
    hpt = hpt_ref[...]
    xcs, beta_cols, gcum_cols, gcum_rows = [], [], [], []
    for sq in range(nseq):
        ub = qkv_ref[sq]
        u = ub.astype(F32)
        prev = tail_ref[sq]
        tail_ref[sq] = u[ts - SUBLANES:, :]
        row8 = lax.broadcasted_iota(jnp.int32, prev.shape, 0)
        acc = u * cw[CONV_K - 1:CONV_K, :]
        head = jnp.zeros(prev.shape, F32)
        for s in range(1, CONV_K):
            w_s = cw[CONV_K - 1 - s:CONV_K - s, :]
            acc = acc + _dot(shift_ref[s - 1], ub) * w_s
            head = head + jnp.where(row8 < s, pltpu.roll(prev, s, axis=0), 0.0) * w_s
        acc = jnp.concatenate([acc[:SUBLANES] + head, acc[SUBLANES:]], axis=0)
        xcs.append(_silu(acc))
        ab = ab_ref[sq]
        g_col = -jnp.exp(hp[0:1, :]) * _softplus(ab + hp[1:2, :])
        beta_cols.append(_sigmoid(ab))
        gcum_cols.append(_seg_cumsum(g_col, ck, 0))
        g_row = -jnp.exp(hpt[:, 0:1]) * _softplus(abt_ref[sq] + hpt[:, 1:2])
        gcum_rows.append(_seg_cumsum(g_row, ck, 1))

    groups = [(h, sq) for h in range(heads) for sq in range(nseq)]
    ng = len(groups)

    def chunks(col0):
        return jnp.concatenate(
            [xcs[sq][:, col0 + h * HEAD_DIM:col0 + (h + 1) * HEAD_DIM].reshape(nc, ck, HEAD_DIM)
             for h, sq in groups], axis=0)

    def col_chunks(cols, lane0):
        return jnp.concatenate(
            [cols[sq][:, lane0 + h:lane0 + h + 1].reshape(nc, ck, 1) for h, sq in groups], axis=0)

    q3, k3, v3 = chunks(0), chunks(width), chunks(2 * width)
    q3 = q3 * lax.rsqrt(jnp.sum(q3 * q3, axis=-1, keepdims=True) + RMS_EPS) * (HEAD_DIM ** -0.5)
    k3 = k3 * lax.rsqrt(jnp.sum(k3 * k3, axis=-1, keepdims=True) + RMS_EPS)
    beta3 = col_chunks(beta_cols, heads)
    gc3 = col_chunks(gcum_cols, 0)
    gr3 = jnp.stack([gcum_rows[sq][h:h + 1, c * ck:(c + 1) * ck]
                     for h, sq in groups for c in range(nc)], axis=0)

    ri = lax.broadcasted_iota(jnp.int32, (ck, ck), 0)
    ci = lax.broadcasted_iota(jnp.int32, (ck, ck), 1)
    incl = ri >= ci
    strict = ri > ci
    decay = jnp.where(incl, jnp.exp(jnp.where(incl, gc3 - gr3, 0.0)), 0.0)
    kb3 = k3 * beta3
    k3b = k3.astype(BF16)
    a_mat = jnp.where(strict, _bdot_nt(kb3.astype(BF16), k3b) * decay, 0.0)
    t_inv = _inv_unit_lower(a_mat)
    exg = jnp.exp(gc3)
    rhs = jnp.concatenate([v3 * beta3, kb3 * exg], axis=2).astype(BF16)
    sol = _bdot(t_inv.astype(BF16), rhs)
    u3, w3 = sol[:, :, :HEAD_DIM], sol[:, :, HEAD_DIM:]
    qk3 = jnp.where(incl, _bdot_nt(q3.astype(BF16), k3b) * decay, 0.0).astype(BF16)
    g_last = gc3[:, ck - 1:ck, :]
    wq3 = jnp.concatenate([w3, q3 * exg], axis=1).astype(BF16)
    kdec3 = (k3 * jnp.exp(g_last - gc3)).astype(BF16)
    gl3 = jnp.exp(g_last)

    by_group = lambda a: a.reshape((ng, nc) + a.shape[1:])
    u4, qk4, wq4, kdec4, gl4 = (by_group(a) for a in (u3, qk3, wq3, kdec3, gl3))
    gn = gn_ref[...]
    state = state_ref[...]
    for c in range(nc):
        ws = _bdot(wq4[:, c], state.astype(BF16))
        v_new = (u4[:, c] - ws[:, :ck]).astype(BF16)
        o = ws[:, ck:] + _bdot(qk4[:, c], v_new)
        state = state * gl4[:, c] + _bdot_tn(kdec4[:, c], v_new)
        o = _rms(o, gn).astype(o_ref.dtype)
        for gi, (h, sq) in enumerate(groups):
            o_ref[sq, c * ck:(c + 1) * ck, h * HEAD_DIM:(h + 1) * HEAD_DIM] = o[gi]
    state_ref[...] = state


def _split_bias(x):
    parts = []
    for _ in range(BIAS_PARTS):
        piece = x.astype(BF16).astype(F32)
        parts.append(piece)
        x = x - piece
    return parts


def _key_ext(bias, extra=None):
    lane = lax.broadcasted_iota(jnp.int32, (bias.shape[0], LANES), 1)
    ext = jnp.zeros((bias.shape[0], LANES), F32) if extra is None else extra
    for i, piece in enumerate(_split_bias(bias)):
        ext = jnp.where(lane == i, piece, ext)
    return ext.astype(BF16)


def _ones_lane0(n):
    lane = lax.broadcasted_iota(jnp.int32, (n, LANES), 1)
    return jnp.where(lane == 0, 1.0, 0.0).astype(BF16)


def _logits(qxs, kx_ref, tk, k0):
    return [_dot_nt(qx, kx_ref[h, pl.ds(k0, tk), :]) for h, qx in enumerate(qxs)]


def _absorb(ss, vx_ref, m_ref, acc_ref, k0, mask, rows):
    width = ss[0].shape[1]
    if mask is not None:
        ss = [jnp.where(mask, s, NEG) for s in ss]
    for h, s in enumerate(ss):
        m_old = m_ref[h, rows, :]
        m_new = jnp.maximum(m_old, jnp.max(s, axis=1, keepdims=True))
        pv = _dot(jnp.exp2(s - m_new).astype(BF16), vx_ref[h, pl.ds(k0, width), :])
        acc_ref[h, rows, :] = jnp.exp2(m_old - m_new) * acc_ref[h, rows, :] + pv
        m_ref[h, rows, :] = m_new


def _attend_causal(qxs, kx_ref, vx_ref, m_ref, acc_ref, s_refs, tk, qi):
    tq = qxs[0].shape[0]
    heads = range(len(qxs))
    n_past = (qi * tq) // tk
    every = slice(0, tq)
    s0_ref, s1_ref = s_refs
    m_ref[...] = jnp.full(m_ref.shape, NEG, F32)
    acc_ref[...] = jnp.zeros(acc_ref.shape, F32)

    def start(t):
        return pl.multiple_of(t * tk, tk)

    def issue(t, s_ref):
        for h, s in enumerate(_logits(qxs, kx_ref, tk, start(t))):
            s_ref[h] = s

    def absorb(t, s_ref):
        _absorb([s_ref[h] for h in heads], vx_ref, m_ref, acc_ref, start(t), None, every)

    odd = n_past % 2

    @pl.when(odd == 1)
    def _():
        _absorb(_logits(qxs, kx_ref, tk, 0), vx_ref, m_ref, acc_ref, 0, None, every)

    issue(odd, s0_ref)

    def pair(u, carry):
        t = odd + 2 * u
        issue(t + 1, s1_ref)
        absorb(t, s0_ref)
        issue(t + 2, s0_ref)
        absorb(t + 1, s1_ref)
        return carry

    lax.fori_loop(0, (n_past - odd) // 2, pair, 0)

    def causal(rows, width, row0):
        return (lax.broadcasted_iota(jnp.int32, (rows, width), 0) + row0
                >= lax.broadcasted_iota(jnp.int32, (rows, width), 1))

    k0 = start(n_past)
    if tq != tk:
        _absorb([s0_ref[h] for h in heads], vx_ref, m_ref, acc_ref, k0,
                causal(tq, tk, qi * tq - n_past * tk), every)
        return
    half = tq // 2
    _absorb([s0_ref[h, :half, :half] for h in heads], vx_ref, m_ref, acc_ref, k0,
            causal(half, half, 0), slice(0, half))
    _absorb([s0_ref[h, half:, :] for h in heads], vx_ref, m_ref, acc_ref, k0,
            causal(half, tk, half), slice(half, tq))


def _softmax_out(acc_ref, h, dtype):
    accx = acc_ref[h]
    return (accx[:, :HEAD_DIM] / accx[:, HEAD_DIM:HEAD_DIM + 1]).astype(dtype)


def _head_cols(hh):
    return slice(hh * HEAD_DIM, (hh + 1) * HEAD_DIM)


def _moba_kernel(slopes_ref, q_ref, k_ref, v_ref, o_ref, kx_ref, vx_ref, kmean_ref, m_ref, acc_ref,
                 s0_ref, s1_ref, *, nb, hp, tk):
    blk = MOBA_BLOCK
    s_len = k_ref.shape[0]
    hg = pl.program_id(1)
    qi = pl.program_id(2)
    nbp = -(-nb // SUBLANES) * SUBLANES

    @pl.when(qi == 0)
    def _():
        pos = lax.broadcasted_iota(jnp.int32, (s_len, 1), 0)
        lane = lax.broadcasted_iota(jnp.int32, (s_len, LANES), 1)
        onehot = jnp.where(lane - SEL_LANE0 == lax.broadcasted_iota(jnp.int32, (s_len, LANES), 0) // blk,
                           1.0, 0.0)
        ones = _ones_lane0(s_len)
        kmean_ref[...] = jnp.zeros_like(kmean_ref)
        for hh in range(hp):
            slope2 = slopes_ref[hg * hp + hh] * LOG2E
            kx_ref[hh, :, :HEAD_DIM] = k_ref[:, _head_cols(hh)]
            kx_ref[hh, :, HEAD_DIM:] = _key_ext(slope2 * pos.astype(F32), onehot)
            vx_ref[hh, :, :HEAD_DIM] = v_ref[:, _head_cols(hh)]
            vx_ref[hh, :, HEAD_DIM:] = ones
            for j in range(nb):
                kmean_ref[hh, j:j + 1, :] = jnp.mean(
                    k_ref[j * blk:(j + 1) * blk, _head_cols(hh)].astype(F32), axis=0, keepdims=True)

    blk_id = lax.broadcasted_iota(jnp.int32, (nbp, blk), 0)
    past = blk_id < qi
    lane = lax.broadcasted_iota(jnp.int32, (blk, LANES), 1)
    qxs = []
    for hh in range(hp):
        q = q_ref[:, _head_cols(hh)]
        gate = sum(_dot_nt(piece.astype(BF16), q) for piece in _split_bias(kmean_ref[hh, :nbp, :]))
        gm = jnp.where(past, gate, NEG)
        rank = jnp.zeros(gm.shape, jnp.int32)
        for j in range(nb):
            cj = gm[j:j + 1, :]
            beats = (cj > gm) | ((cj == gm) & (blk_id > j))
            rank = rank + beats.astype(jnp.int32)
        sel_t = jnp.where(((rank < MOBA_TOPK) & past) | (blk_id == qi), 0.0, NEG)
        sel_t = jnp.concatenate([jnp.zeros((SEL_LANE0, blk), F32), sel_t,
                                 jnp.zeros((LANES - SEL_LANE0 - nbp, blk), F32)], axis=0)
        qext = jnp.where(lane < BIAS_PARTS, 1.0, sel_t.T).astype(BF16)
        qxs.append(jnp.concatenate([q, qext], axis=1))

    _attend_causal(qxs, kx_ref, vx_ref, m_ref, acc_ref, (s0_ref, s1_ref), tk, qi)
    for hh in range(hp):
        o_ref[:, _head_cols(hh)] = _softmax_out(acc_ref, hh, o_ref.dtype)


def _fox_kernel(q_ref, k_ref, v_ref, c_ref, o_ref, kx_ref, vx_ref, m_ref, acc_ref, s0_ref, s1_ref,
                *, hp, tk):
    tq = q_ref.shape[0]
    s_len = k_ref.shape[0]
    hg = pl.program_id(1)
    qi = pl.program_id(2)

    @pl.when(qi == 0)
    def _():
        lane = lax.broadcasted_iota(jnp.int32, (s_len, LANES), 1)
        ones = _ones_lane0(s_len)
        for hh in range(hp):
            c_col = jnp.sum(jnp.where(lane == hg * hp + hh, c_ref[...], 0.0), axis=1, keepdims=True)
            kx_ref[hh, :, :HEAD_DIM] = k_ref[:, _head_cols(hh)]
            kx_ref[hh, :, HEAD_DIM:] = _key_ext(-LOG2E * c_col)
            vx_ref[hh, :, :HEAD_DIM] = v_ref[:, _head_cols(hh)]
            vx_ref[hh, :, HEAD_DIM:] = ones

    lane = lax.broadcasted_iota(jnp.int32, (tq, LANES), 1)
    qext = jnp.where(lane < BIAS_PARTS, 1.0, 0.0).astype(BF16)
    qxs = [jnp.concatenate([q_ref[:, _head_cols(hh)], qext], axis=1) for hh in range(hp)]

    _attend_causal(qxs, kx_ref, vx_ref, m_ref, acc_ref, (s0_ref, s1_ref), tk, qi)
    for hh in range(hp):
        o_ref[:, _head_cols(hh)] = _softmax_out(acc_ref, hh, o_ref.dtype)


def _pad_cols(w, n):
    return jnp.pad(w, ((0, 0), (0, n - w.shape[1])))


def kernel(x, p, norm_g, w_in_ab, conv_w, a_log, dt_bias, gdn_norm_g, w_out_ab, w_in_c, forget_b,
           w_out_c, ple_norm_g, w_ple_gate, w_ple_proj, final_g):
    b, s, d = x.shape
    t = b * s
    gdn_heads = a_log.shape[1]
    gdn_w = gdn_heads * HEAD_DIM
    moba_w = (w_out_ab.shape[1] - gdn_w)
    moba_heads = moba_w // HEAD_DIM
    fox_heads = forget_b.shape[1]
    fox_w = fox_heads * HEAD_DIM
    mix_ab = gdn_w + moba_w
    assert s % MOBA_BLOCK == 0 and s % ROW_TILE == 0 and s % GDN_TILE == 0 and s % FOX_TQ == 0
    assert 2 * gdn_heads <= SUBLANES and fox_heads <= LANES
    tm = ROW_TILE
    n_row = t // tm

    xf = x.reshape(t, d)
    pf = p.reshape(p.shape[0], t, PLE_DIM)
    row = lambda v: v.reshape(1, -1).astype(F32)

    w0 = w_in_ab[0]
    c1 = 3 * gdn_w
    c2 = c1 + 2 * gdn_heads
    w0_big = jnp.concatenate([w0[:, :c1], w0[:, c2:c2 + moba_w] * ATT_SCALE, w0[:, c2 + moba_w:]],
                             axis=1).astype(BF16)
    w0_small = w0[:, c1:c2]
    w0_s = _pad_cols(w0_small, LANES).astype(BF16)
    w0_st = jnp.pad(w0_small.T, ((0, SMALL_ROWS - 2 * gdn_heads), (0, 0))).astype(BF16)

    qkv_a, qkv_b, z0, ab, abt = pl.pallas_call(
        _rows_a_kernel,
        grid=(n_row,),
        in_specs=[_row_spec(tm, d), _const_spec((1, d)), _const_spec(w0_big.shape),
                  _const_spec(w0_s.shape), _const_spec(w0_st.shape)],
        out_specs=[_row_spec(tm, c1), _row_spec(tm, 3 * moba_w), _row_spec(tm, mix_ab),
                   _row_spec(tm, LANES),
                   pl.BlockSpec((None, SUBLANES, tm), lambda i: (i // (s // tm), 0, i % (s // tm)))],
        out_shape=[jax.ShapeDtypeStruct((t, c1), BF16), jax.ShapeDtypeStruct((t, 3 * moba_w), BF16),
                   jax.ShapeDtypeStruct((t, mix_ab), BF16), jax.ShapeDtypeStruct((t, LANES), F32),
                   jax.ShapeDtypeStruct((b, SUBLANES, s), F32)],
        compiler_params=_row_params("parallel"),
        name="rows_a",
    )(xf, row(norm_g[0]), w0_big, w0_s, w0_st)

    ts = GDN_TILE
    n_seq = s // ts
    nsq = GDN_SEQS_PER_STEP if b % GDN_SEQS_PER_STEP == 0 else 1
    tpos = jnp.arange(ts)
    shifts = jnp.stack([(tpos[:, None] - tpos[None, :] == sh).astype(BF16) for sh in range(1, CONV_K)])
    hp = jnp.zeros((SUBLANES, LANES), F32)
    hp = hp.at[0, :gdn_heads].set(a_log[0]).at[1, :gdn_heads].set(dt_bias[0])
    hpt = jnp.zeros((SUBLANES, LANES), F32)
    hpt = hpt.at[:gdn_heads, 0].set(a_log[0]).at[:gdn_heads, 1].set(dt_bias[0])
    oa = pl.pallas_call(
        functools.partial(_gdn_kernel, heads=gdn_heads),
        grid=(b // nsq, n_seq),
        in_specs=[pl.BlockSpec((nsq, ts, c1), lambda i, j: (i, j, 0)),
                  pl.BlockSpec((nsq, ts, LANES), lambda i, j: (i, j, 0)),
                  pl.BlockSpec((nsq, SUBLANES, ts), lambda i, j: (i, 0, j)),
                  pl.BlockSpec((CONV_K, c1), lambda i, j: (0, 0)),
                  pl.BlockSpec((CONV_K - 1, ts, ts), lambda i, j: (0, 0, 0)),
                  pl.BlockSpec((SUBLANES, LANES), lambda i, j: (0, 0)),
                  pl.BlockSpec((SUBLANES, LANES), lambda i, j: (0, 0)),
                  pl.BlockSpec((1, HEAD_DIM), lambda i, j: (0, 0))],
        out_specs=pl.BlockSpec((nsq, ts, gdn_w), lambda i, j: (i, j, 0)),
        out_shape=jax.ShapeDtypeStruct((b, s, gdn_w), BF16),
        scratch_shapes=[pltpu.VMEM((gdn_heads * nsq, HEAD_DIM, HEAD_DIM), F32),
                        pltpu.VMEM((nsq, SUBLANES, c1), F32)],
        compiler_params=pltpu.CompilerParams(dimension_semantics=("parallel", "arbitrary"),
                                             vmem_limit_bytes=VMEM_LIMIT),
        name="gdn",
    )(qkv_a.reshape(b, s, c1), ab.reshape(b, s, LANES), abt, conv_w[0].astype(F32), shifts, hp, hpt,
      row(gdn_norm_g[0]))
    oa = oa.reshape(t, gdn_w)

    nb = s // MOBA_BLOCK
    hpm = min(MOBA_HEADS_PER_STEP, moba_heads)
    assert moba_heads % hpm == 0 and SEL_LANE0 + nb <= LANES and s % MOBA_TK == 0
    slopes = 2.0 ** (-8.0 * (jnp.arange(moba_heads, dtype=F32) + 1.0) / moba_heads)
    qkv_b3 = qkv_b.reshape(b, s, 3 * moba_w)
    mg = moba_heads // hpm
    ob = pl.pallas_call(
        functools.partial(_moba_kernel, nb=nb, hp=hpm, tk=MOBA_TK),
        grid=(b, mg, nb),
        in_specs=[pl.BlockSpec(memory_space=pltpu.SMEM),
                  pl.BlockSpec((None, MOBA_BLOCK, hpm * HEAD_DIM), lambda i, h, j: (i, j, h)),
                  pl.BlockSpec((None, s, hpm * HEAD_DIM), lambda i, h, j: (i, 0, mg + h)),
                  pl.BlockSpec((None, s, hpm * HEAD_DIM), lambda i, h, j: (i, 0, 2 * mg + h))],
        out_specs=pl.BlockSpec((None, MOBA_BLOCK, hpm * HEAD_DIM), lambda i, h, j: (i, j, h)),
        out_shape=jax.ShapeDtypeStruct((b, s, moba_w), BF16),
        scratch_shapes=[pltpu.VMEM((hpm, s, HEAD_DIM + LANES), BF16),
                        pltpu.VMEM((hpm, s, HEAD_DIM + LANES), BF16),
                        pltpu.VMEM((hpm, LANES, HEAD_DIM), F32),
                        pltpu.VMEM((hpm, MOBA_BLOCK, 1), F32),
                        pltpu.VMEM((hpm, MOBA_BLOCK, HEAD_DIM + LANES), F32),
                        pltpu.VMEM((hpm, MOBA_BLOCK, MOBA_TK), F32),
                        pltpu.VMEM((hpm, MOBA_BLOCK, MOBA_TK), F32)],
        compiler_params=pltpu.CompilerParams(
            dimension_semantics=("parallel", "parallel", "arbitrary"), vmem_limit_bytes=VMEM_LIMIT),
        name="moba",
    )(slopes, qkv_b3, qkv_b3, qkv_b3)
    ob = ob.reshape(t, moba_w)

    w1 = w_in_c[0]
    e1 = 3 * fox_w
    e2 = e1 + fox_heads
    w1_big = jnp.concatenate([w1[:, :fox_w] * ATT_SCALE, w1[:, fox_w:e1], w1[:, e2:]],
                             axis=1).astype(BF16)
    w1_f = _pad_cols(w1[:, e1:e2], LANES).astype(BF16)
    fb = _pad_cols(row(forget_b[0]), LANES)
    ple_w = lambda i: (w_ple_gate[i].astype(BF16), w_ple_proj[i].astype(BF16))
    wg0, wp0 = ple_w(0)
    wout0 = w_out_ab[0].astype(BF16)
    x2, qkv_c, z1, cfox = pl.pallas_call(
        functools.partial(_rows_d_kernel, tiles_per_seq=s // tm),
        grid=(n_row,),
        in_specs=[_row_spec(tm, gdn_w), _row_spec(tm, moba_w), _row_spec(tm, mix_ab), _row_spec(tm, d),
                  _layer_spec(tm, PLE_DIM, 0), _const_spec(wout0.shape), _const_spec((1, d)),
                  _const_spec(wg0.shape), _const_spec(wp0.shape), _const_spec((1, d)),
                  _const_spec(w1_big.shape), _const_spec(w1_f.shape), _const_spec(fb.shape)],
        out_specs=[_row_spec(tm, d), _row_spec(tm, e1), _row_spec(tm, fox_w), _row_spec(tm, LANES)],
        out_shape=[jax.ShapeDtypeStruct((t, d), F32), jax.ShapeDtypeStruct((t, e1), BF16),
                   jax.ShapeDtypeStruct((t, fox_w), BF16), jax.ShapeDtypeStruct((t, LANES), F32)],
        scratch_shapes=[pltpu.VMEM((1, LANES), F32)],
        compiler_params=_row_params("arbitrary"),
        name="rows_d",
    )(oa, ob, z0, xf, pf, wout0, row(ple_norm_g[0]), wg0, wp0, row(norm_g[1]), w1_big, w1_f, fb)

    tq, tk = FOX_TQ, FOX_TK
    hpf = min(FOX_HEADS_PER_STEP, fox_heads)
    assert fox_heads % hpf == 0 and tk % tq == 0 and s % tk == 0
    fg = fox_heads // hpf
    qkv_c3 = qkv_c.reshape(b, s, e1)
    oc = pl.pallas_call(
        functools.partial(_fox_kernel, hp=hpf, tk=tk),
        grid=(b, fg, s // tq),
        in_specs=[pl.BlockSpec((None, tq, hpf * HEAD_DIM), lambda i, h, j: (i, j, h)),
                  pl.BlockSpec((None, s, hpf * HEAD_DIM), lambda i, h, j: (i, 0, fg + h)),
                  pl.BlockSpec((None, s, hpf * HEAD_DIM), lambda i, h, j: (i, 0, 2 * fg + h)),
                  pl.BlockSpec((s, LANES), lambda i, h, j: (i, 0))],
        out_specs=pl.BlockSpec((None, tq, hpf * HEAD_DIM), lambda i, h, j: (i, j, h)),
        out_shape=jax.ShapeDtypeStruct((b, s, fox_w), BF16),
        scratch_shapes=[pltpu.VMEM((hpf, s, HEAD_DIM + LANES), BF16),
                        pltpu.VMEM((hpf, s, HEAD_DIM + LANES), BF16),
                        pltpu.VMEM((hpf, tq, 1), F32),
                        pltpu.VMEM((hpf, tq, HEAD_DIM + LANES), F32),
                        pltpu.VMEM((hpf, tq, tk), F32),
                        pltpu.VMEM((hpf, tq, tk), F32)],
        compiler_params=pltpu.CompilerParams(
            dimension_semantics=("parallel", "parallel", "arbitrary"), vmem_limit_bytes=VMEM_LIMIT),
        name="fox",
    )(qkv_c3, qkv_c3, qkv_c3, cfox)
    oc = oc.reshape(t, fox_w)

    wg1, wp1 = ple_w(1)
    wout1 = w_out_c[0].astype(BF16)
    out = pl.pallas_call(
        _rows_f_kernel,
        grid=(n_row,),
        in_specs=[_row_spec(tm, fox_w), _row_spec(tm, fox_w), _row_spec(tm, d), _layer_spec(tm, PLE_DIM, 1),
                  _const_spec(wout1.shape), _const_spec((1, d)), _const_spec(wg1.shape),
                  _const_spec(wp1.shape), _const_spec((1, d))],
        out_specs=_row_spec(tm, d),
        out_shape=jax.ShapeDtypeStruct((t, d), F32),
        compiler_params=_row_params("parallel"),
        name="rows_f",
    )(oc, z1, x2, pf, wout1, row(ple_norm_g[1]), wg1, wp1, row(final_g))
    return out.reshape(b, s, d)
```

```python
import functools

import jax
import jax.numpy as jnp
from jax import lax
from jax.experimental import pallas as pl
from jax.experimental.pallas import tpu as pltpu

F32 = jnp.float32
BF16 = jnp.bfloat16

HEAD_DIM = 128
PLE_DIM = 256
CONV_K = 4
GDN_CHUNK = 64
MOBA_BLOCK = 256
MOBA_TOPK = 3
RMS_EPS = 1e-6
NEG = -(2.0 ** 100)
LANES = 128
SUBLANES = 8
SMALL_ROWS = 16
VMEM_LIMIT = 56 * 1024 * 1024

LOG2E = 1.4426950408889634
ATT_SCALE = HEAD_DIM ** -0.5 * LOG2E

ROW_TILE = 512
PROJ_CHUNK = 512
GDN_TILE = 256
GDN_SEQS_PER_STEP = 4
ATT_TQ = 1024
MOBA_HEADS_PER_STEP = 1
FOX_HEADS_PER_STEP = 1
BIAS_PARTS = 3
SEL_LANE0 = 8


def _sigmoid(x):
    return 1.0 / (1.0 + jnp.exp(-x))


def _silu(x):
    return x * _sigmoid(x)


def _softplus(x):
    return jnp.maximum(x, 0.0) + jnp.log1p(jnp.exp(-jnp.abs(x)))


def _log_sigmoid(x):
    return jnp.minimum(x, 0.0) - jnp.log1p(jnp.exp(-jnp.abs(x)))


def _rms(xf, g):
    return xf * lax.rsqrt(jnp.mean(xf * xf, axis=-1, keepdims=True) + RMS_EPS) * g


def _dot(a, b):
    return jnp.dot(a, b, preferred_element_type=F32)


def _dot_nt(a, b):
    return lax.dot_general(a, b, (((1,), (1,)), ((), ())), preferred_element_type=F32)


def _seg_cumsum(v, seg, axis):
    pos = lax.broadcasted_iota(jnp.int32, v.shape, axis) & (seg - 1)
    s = 1
    while s < seg:
        v = v + jnp.where(pos >= s, pltpu.roll(v, s, axis=axis), 0.0)
        s *= 2
    return v


def _mix_ple(o_parts, z, x, p, w_out_ref, gple, w_gate_ref, w_proj_ref):
    gz = _silu(z.astype(F32))
    acc = x
    off = 0
    for o in o_parts:
        w = o.shape[1]
        y = (o.astype(F32) * gz[:, off:off + w]).astype(BF16)
        acc = acc + _dot(y, w_out_ref[off:off + w, :])
        off += w
    hn = _rms(acc, gple).astype(BF16)
    gate = _sigmoid(_dot(hn, w_gate_ref[...]))
    pp = _dot(p.astype(BF16), w_proj_ref[...])
    return acc + gate * pp


def _project(hn, w_ref, out_refs):
    off = 0
    for o_ref in out_refs:
        n = o_ref.shape[1]
        for c0 in range(0, n, PROJ_CHUNK):
            c1 = min(n, c0 + PROJ_CHUNK)
            o_ref[:, c0:c1] = _dot(hn, w_ref[:, off + c0:off + c1]).astype(o_ref.dtype)
        off += n


def _rows_a_kernel(x_ref, g_ref, w_ref, ws_ref, wst_ref,
                   qkva_ref, qkvb_ref, z_ref, ab_ref, abt_ref):
    hn = _rms(x_ref[...], g_ref[...]).astype(BF16)
    _project(hn, w_ref, (qkva_ref, qkvb_ref, z_ref))
    ab_ref[...] = _dot(hn, ws_ref[...])
    abt_ref[...] = _dot_nt(wst_ref[...], hn)[:SUBLANES]


def _rows_d_kernel(oa_ref, ob_ref, z_ref, x_ref, p_ref, wout_ref, gple_ref, wgate_ref, wproj_ref,
                   g_ref, w_ref, wf_ref, fb_ref,
                   x2_ref, qkv_ref, z1_ref, c_ref, carry_ref, *, tiles_per_seq):
    x2 = _mix_ple((oa_ref[...], ob_ref[...]), z_ref[...], x_ref[...], p_ref[...],
                  wout_ref, gple_ref[...], wgate_ref, wproj_ref)
    x2_ref[...] = x2
    hn = _rms(x2, g_ref[...]).astype(BF16)
    _project(hn, w_ref, (qkv_ref, z1_ref))

    @pl.when(pl.program_id(0) % tiles_per_seq == 0)
    def _():
        carry_ref[...] = jnp.zeros_like(carry_ref)

    log_f = _log_sigmoid(_dot(hn, wf_ref[...]) + fb_ref[...])
    tm = log_f.shape[0]
    c = _seg_cumsum(log_f, tm, 0) + carry_ref[...]
    c_ref[...] = c
    carry_ref[...] = c[tm - 1:tm, :]


def _rows_f_kernel(o_ref, z_ref, x_ref, p_ref, wout_ref, gple_ref, wgate_ref, wproj_ref,
                   g_ref, out_ref):
    x2 = _mix_ple((o_ref[...],), z_ref[...], x_ref[...], p_ref[...],
                  wout_ref, gple_ref[...], wgate_ref, wproj_ref)
    out_ref[...] = _rms(x2, g_ref[...])


def _row_spec(tm, n):
    return pl.BlockSpec((tm, n), lambda i: (i, 0))


def _layer_spec(tm, n, layer):
    return pl.BlockSpec((None, tm, n), lambda i: (layer, i, 0))


def _const_spec(shape):
    return pl.BlockSpec(shape, lambda i: (0,) * len(shape), pipeline_mode=pl.Buffered(1))


def _row_params(semantics):
    return pltpu.CompilerParams(dimension_semantics=(semantics,), vmem_limit_bytes=VMEM_LIMIT)


def _bdot(a, b):
    return lax.dot_general(a, b, (((2,), (1,)), ((0,), (0,))), preferred_element_type=F32)


def _bdot_nt(a, b):
    return lax.dot_general(a, b, (((2,), (2,)), ((0,), (0,))), preferred_element_type=F32)


def _bdot_tn(a, b):
    return lax.dot_general(a, b, (((1,), (1,)), ((0,), (0,))), preferred_element_type=F32)


def _inv_unit_lower(a):
    c = a.shape[-1]
    eye = (lax.broadcasted_iota(jnp.int32, (c, c), 0)
           == lax.broadcasted_iota(jnp.int32, (c, c), 1)).astype(F32)
    t = eye - a
    p = a
    k = 2
    while k < c:
        pb = p.astype(BF16)
        p = _bdot(pb, pb)
        t = t + _bdot(t.astype(BF16), p.astype(BF16))
        k *= 2
    return t


def _gdn_kernel(qkv_ref, ab_ref, abt_ref, cw_ref, shift_ref, hp_ref, hpt_ref, gn_ref,
                o_ref, state_ref, tail_ref, *, heads):
    nseq, ts = qkv_ref.shape[0], qkv_ref.shape[1]
    width = heads * HEAD_DIM
    ck = GDN_CHUNK
    nc = ts // ck

    @pl.when(pl.program_id(1) == 0)
    def _():
        state_ref[...] = jnp.zeros_like(state_ref)
        tail_ref[...] = jnp.zeros_like(tail_ref)

    cw = cw_ref[...]
    hp = hp_ref[...]
    hpt = hpt_ref[...]
    xcs, beta_cols, gcum_cols, gcum_rows = [], [], [], []
    for sq in range(nseq):
        ub = qkv_ref[sq]
        u = ub.astype(F32)
        prev = tail_ref[sq]
        tail_ref[sq] = u[ts - SUBLANES:, :]
        row8 = lax.broadcasted_iota(jnp.int32, prev.shape, 0)
        acc = u * cw[CONV_K - 1:CONV_K, :]
        head = jnp.zeros(prev.shape, F32)
        for s in range(1, CONV_K):
            w_s = cw[CONV_K - 1 - s:CONV_K - s, :]
            acc = acc + _dot(shift_ref[s - 1], ub) * w_s
            head = head + jnp.where(row8 < s, pltpu.roll(prev, s, axis=0), 0.0) * w_s
        acc = jnp.concatenate([acc[:SUBLANES] + head, acc[SUBLANES:]], axis=0)
        xcs.append(_silu(acc))
        ab = ab_ref[sq]
        g_col = -jnp.exp(hp[0:1, :]) * _softplus(ab + hp[1:2, :])
        beta_cols.append(_sigmoid(ab))
        gcum_cols.append(_seg_cumsum(g_col, ck, 0))
        g_row = -jnp.exp(hpt[:, 0:1]) * _softplus(abt_ref[sq] + hpt[:, 1:2])
        gcum_rows.append(_seg_cumsum(g_row, ck, 1))

    groups = [(h, sq) for h in range(heads) for sq in range(nseq)]
    ng = len(groups)

    def chunks(col0):
        return jnp.concatenate(
            [xcs[sq][:, col0 + h * HEAD_DIM:col0 + (h + 1) * HEAD_DIM].reshape(nc, ck, HEAD_DIM)
             for h, sq in groups], axis=0)

    def col_chunks(cols, lane0):
        return jnp.concatenate(
            [cols[sq][:, lane0 + h:lane0 + h + 1].reshape(nc, ck, 1) for h, sq in groups], axis=0)

    q3, k3, v3 = chunks(0), chunks(width), chunks(2 * width)
    q3 = q3 * lax.rsqrt(jnp.sum(q3 * q3, axis=-1, keepdims=True) + RMS_EPS) * (HEAD_DIM ** -0.5)
    k3 = k3 * lax.rsqrt(jnp.sum(k3 * k3, axis=-1, keepdims=True) + RMS_EPS)
    beta3 = col_chunks(beta_cols, heads)
    gc3 = col_chunks(gcum_cols, 0)
    gr3 = jnp.stack([gcum_rows[sq][h:h + 1, c * ck:(c + 1) * ck]
                     for h, sq in groups for c in range(nc)], axis=0)

    ri = lax.broadcasted_iota(jnp.int32, (ck, ck), 0)
    ci = lax.broadcasted_iota(jnp.int32, (ck, ck), 1)
    incl = ri >= ci
    strict = ri > ci
    decay = jnp.where(incl, jnp.exp(jnp.where(incl, gc3 - gr3, 0.0)), 0.0)
    kb3 = k3 * beta3
    k3b = k3.astype(BF16)
    a_mat = jnp.where(strict, _bdot_nt(kb3.astype(BF16), k3b) * decay, 0.0)
    t_inv = _inv_unit_lower(a_mat)
    exg = jnp.exp(gc3)
    rhs = jnp.concatenate([v3 * beta3, kb3 * exg], axis=2).astype(BF16)
    sol = _bdot(t_inv.astype(BF16), rhs)
    u3, w3 = sol[:, :, :HEAD_DIM], sol[:, :, HEAD_DIM:]
    qk3 = jnp.where(incl, _bdot_nt(q3.astype(BF16), k3b) * decay, 0.0).astype(BF16)
    g_last = gc3[:, ck - 1:ck, :]
    wq3 = jnp.concatenate([w3, q3 * exg], axis=1).astype(BF16)
    kdec3 = (k3 * jnp.exp(g_last - gc3)).astype(BF16)
    gl3 = jnp.exp(g_last)

    by_group = lambda a: a.reshape((ng, nc) + a.shape[1:])
    u4, qk4, wq4, kdec4, gl4 = (by_group(a) for a in (u3, qk3, wq3, kdec3, gl3))
    gn = gn_ref[...]
    state = state_ref[...]
    for c in range(nc):
        ws = _bdot(wq4[:, c], state.astype(BF16))
        v_new = (u4[:, c] - ws[:, :ck]).astype(BF16)
        o = ws[:, ck:] + _bdot(qk4[:, c], v_new)
        state = state * gl4[:, c] + _bdot_tn(kdec4[:, c], v_new)
        o = _rms(o, gn).astype(o_ref.dtype)
        for gi, (h, sq) in enumerate(groups):
            o_ref[sq, c * ck:(c + 1) * ck, h * HEAD_DIM:(h + 1) * HEAD_DIM] = o[gi]
    state_ref[...] = state


def _split_bias(x):
    parts = []
    for _ in range(BIAS_PARTS):
        piece = x.astype(BF16).astype(F32)
        parts.append(piece)
        x = x - piece
    return parts


def _key_ext(bias, extra=None):
    lane = lax.broadcasted_iota(jnp.int32, (bias.shape[0], LANES), 1)
    ext = jnp.zeros((bias.shape[0], LANES), F32) if extra is None else extra
    for i, piece in enumerate(_split_bias(bias)):
        ext = jnp.where(lane == i, piece, ext)
    return ext.astype(BF16)


def _ones_lane0(n):
    lane = lax.broadcasted_iota(jnp.int32, (n, LANES), 1)
    return jnp.where(lane == 0, 1.0, 0.0).astype(BF16)


def _logits(qxs, kx_ref, tk, k0):
    return [_dot_nt(qx, kx_ref[h, pl.ds(k0, tk), :]) for h, qx in enumerate(qxs)]


def _absorb(ss, vx_ref, m_ref, acc_ref, k0, mask, rows):
    width = ss[0].shape[1]
    if mask is not None:
        ss = [jnp.where(mask, s, NEG) for s in ss]
    for h, s in enumerate(ss):
        m_old = m_ref[h, rows, :]
        m_new = jnp.maximum(m_old, jnp.max(s, axis=1, keepdims=True))
        pv = _dot(jnp.exp2(s - m_new).astype(BF16), vx_ref[h, pl.ds(k0, width), :])
        acc_ref[h, rows, :] = jnp.exp2(m_old - m_new) * acc_ref[h, rows, :] + pv
        m_ref[h, rows, :] = m_new


def _softmax_out(acc_ref, h, dtype):
    accx = acc_ref[h]
    return (accx[:, :HEAD_DIM] / accx[:, HEAD_DIM:HEAD_DIM + 1]).astype(dtype)


def _head_cols(hh):
    return slice(hh * HEAD_DIM, (hh + 1) * HEAD_DIM)


def _causal_triangle(make_qxs, kx_ref, vx_ref, m_ref, acc_ref, o_ref, tq, hp):
    s_len = o_ref.shape[0]
    every = slice(0, tq)
    half = tq // 2

    def causal(rows, width, row0):
        return (lax.broadcasted_iota(jnp.int32, (rows, width), 0) + row0
                >= lax.broadcasted_iota(jnp.int32, (rows, width), 1))

    for qi in range(s_len // tq):
        rows = slice(qi * tq, (qi + 1) * tq)
        m_q, acc_q = m_ref.at[qi % 2], acc_ref.at[qi % 2]
        qxs = make_qxs(qi, rows)
        m_q[...] = jnp.full(m_q.shape, NEG, F32)
        acc_q[...] = jnp.zeros(acc_q.shape, F32)
        for t in range(qi):
            _absorb(_logits(qxs, kx_ref, tq, t * tq), vx_ref, m_q, acc_q, t * tq, None, every)
        k0 = qi * tq
        _absorb(_logits([qx[:half] for qx in qxs], kx_ref, half, k0), vx_ref, m_q, acc_q, k0,
                causal(half, half, 0), slice(0, half))
        _absorb(_logits([qx[half:] for qx in qxs], kx_ref, tq, k0), vx_ref, m_q, acc_q, k0,
                causal(half, tq, half), slice(half, tq))
        for hh in range(hp):
            o_ref[rows, _head_cols(hh)] = _softmax_out(acc_q, hh, o_ref.dtype)


def _moba_kernel(slopes_ref, q_ref, k_ref, v_ref, o_ref, kx_ref, vx_ref, kmean_ref, m_ref, acc_ref,
                 *, nb, hp, tq):
    blk = MOBA_BLOCK
    s_len = k_ref.shape[0]
    hg = pl.program_id(1)
    nbp = -(-nb // SUBLANES) * SUBLANES

    pos = lax.broadcasted_iota(jnp.int32, (s_len, 1), 0)
    lane_s = lax.broadcasted_iota(jnp.int32, (s_len, LANES), 1)
    onehot = jnp.where(lane_s - SEL_LANE0 == lax.broadcasted_iota(jnp.int32, (s_len, LANES), 0) // blk,
                       1.0, 0.0)
    ones = _ones_lane0(s_len)
    kmean_ref[...] = jnp.zeros_like(kmean_ref)
    for hh in range(hp):
        slope2 = slopes_ref[hg * hp + hh] * LOG2E
        kx_ref[hh, :, :HEAD_DIM] = k_ref[:, _head_cols(hh)]
        kx_ref[hh, :, HEAD_DIM:] = _key_ext(slope2 * pos.astype(F32), onehot)
        vx_ref[hh, :, :HEAD_DIM] = v_ref[:, _head_cols(hh)]
        vx_ref[hh, :, HEAD_DIM:] = ones
        for j in range(nb):
            kmean_ref[hh, j:j + 1, :] = jnp.mean(
                k_ref[j * blk:(j + 1) * blk, _head_cols(hh)].astype(F32), axis=0, keepdims=True)

    blk_id = lax.broadcasted_iota(jnp.int32, (nbp, tq), 0)
    lane = lax.broadcasted_iota(jnp.int32, (tq, LANES), 1)

    def make_qxs(qi, rows):
        q_blk = qi * (tq // blk) + lax.broadcasted_iota(jnp.int32, (nbp, tq), 1) // blk
        past = blk_id < q_blk
        qxs = []
        for hh in range(hp):
            q = q_ref[rows, _head_cols(hh)]
            gate = sum(_dot_nt(piece.astype(BF16), q) for piece in _split_bias(kmean_ref[hh, :nbp, :]))
            gm = jnp.where(past, gate, NEG)
            rank = jnp.zeros(gm.shape, jnp.int32)
            for j in range(nb):
                cj = gm[j:j + 1, :]
                beats = (cj > gm) | ((cj == gm) & (blk_id > j))
                rank = rank + beats.astype(jnp.int32)
            sel_t = jnp.where(((rank < MOBA_TOPK) & past) | (blk_id == q_blk), 0.0, NEG)
            sel_t = jnp.concatenate([jnp.zeros((SEL_LANE0, tq), F32), sel_t,
                                     jnp.zeros((LANES - SEL_LANE0 - nbp, tq), F32)], axis=0)
            qext = jnp.where(lane < BIAS_PARTS, 1.0, sel_t.T).astype(BF16)
            qxs.append(jnp.concatenate([q, qext], axis=1))
        return qxs

    _causal_triangle(make_qxs, kx_ref, vx_ref, m_ref, acc_ref, o_ref, tq, hp)


def _fox_kernel(q_ref, k_ref, v_ref, c_ref, o_ref, kx_ref, vx_ref, m_ref, acc_ref, *, hp, tq):
    s_len = k_ref.shape[0]
    hg = pl.program_id(1)
    lane_s = lax.broadcasted_iota(jnp.int32, (s_len, LANES), 1)
    ones = _ones_lane0(s_len)
    for hh in range(hp):
        c_col = jnp.sum(jnp.where(lane_s == hg * hp + hh, c_ref[...], 0.0), axis=1, keepdims=True)
        kx_ref[hh, :, :HEAD_DIM] = k_ref[:, _head_cols(hh)]
        kx_ref[hh, :, HEAD_DIM:] = _key_ext(-LOG2E * c_col)
        vx_ref[hh, :, :HEAD_DIM] = v_ref[:, _head_cols(hh)]
        vx_ref[hh, :, HEAD_DIM:] = ones

    lane = lax.broadcasted_iota(jnp.int32, (tq, LANES), 1)
    qext = jnp.where(lane < BIAS_PARTS, 1.0, 0.0).astype(BF16)

    def make_qxs(qi, rows):
        return [jnp.concatenate([q_ref[rows, _head_cols(hh)], qext], axis=1) for hh in range(hp)]

    _causal_triangle(make_qxs, kx_ref, vx_ref, m_ref, acc_ref, o_ref, tq, hp)


def _pad_cols(w, n):
    return jnp.pad(w, ((0, 0), (0, n - w.shape[1])))


def kernel(x, p, norm_g, w_in_ab, conv_w, a_log, dt_bias, gdn_norm_g, w_out_ab, w_in_c, forget_b,
           w_out_c, ple_norm_g, w_ple_gate, w_ple_proj, final_g):
    b, s, d = x.shape
    t = b * s
    gdn_heads = a_log.shape[1]
    gdn_w = gdn_heads * HEAD_DIM
    moba_w = (w_out_ab.shape[1] - gdn_w)
    moba_heads = moba_w // HEAD_DIM
    fox_heads = forget_b.shape[1]
    fox_w = fox_heads * HEAD_DIM
    mix_ab = gdn_w + moba_w
    assert s % MOBA_BLOCK == 0 and s % ROW_TILE == 0 and s % GDN_TILE == 0
    assert 2 * gdn_heads <= SUBLANES and fox_heads <= LANES
    tm = ROW_TILE
    n_row = t // tm

    xf = x.reshape(t, d)
    pf = p.reshape(p.shape[0], t, PLE_DIM)
    row = lambda v: v.reshape(1, -1).astype(F32)

    w0 = w_in_ab[0]
    c1 = 3 * gdn_w
    c2 = c1 + 2 * gdn_heads
    w0_big = jnp.concatenate([w0[:, :c1], w0[:, c2:c2 + moba_w] * ATT_SCALE, w0[:, c2 + moba_w:]],
                             axis=1).astype(BF16)
    w0_small = w0[:, c1:c2]
    w0_s = _pad_cols(w0_small, LANES).astype(BF16)
    w0_st = jnp.pad(w0_small.T, ((0, SMALL_ROWS - 2 * gdn_heads), (0, 0))).astype(BF16)

    qkv_a, qkv_b, z0, ab, abt = pl.pallas_call(
        _rows_a_kernel,
        grid=(n_row,),
        in_specs=[_row_spec(tm, d), _const_spec((1, d)), _const_spec(w0_big.shape),
                  _const_spec(w0_s.shape), _const_spec(w0_st.shape)],
        out_specs=[_row_spec(tm, c1), _row_spec(tm, 3 * moba_w), _row_spec(tm, mix_ab),
                   _row_spec(tm, LANES),
                   pl.BlockSpec((None, SUBLANES, tm), lambda i: (i // (s // tm), 0, i % (s // tm)))],
        out_shape=[jax.ShapeDtypeStruct((t, c1), BF16), jax.ShapeDtypeStruct((t, 3 * moba_w), BF16),
                   jax.ShapeDtypeStruct((t, mix_ab), BF16), jax.ShapeDtypeStruct((t, LANES), F32),
                   jax.ShapeDtypeStruct((b, SUBLANES, s), F32)],
        compiler_params=_row_params("parallel"),
        name="rows_a",
    )(xf, row(norm_g[0]), w0_big, w0_s, w0_st)

    ts = GDN_TILE
    n_seq = s // ts
    nsq = GDN_SEQS_PER_STEP if b % GDN_SEQS_PER_STEP == 0 else 1
    tpos = jnp.arange(ts)
    shifts = jnp.stack([(tpos[:, None] - tpos[None, :] == sh).astype(BF16) for sh in range(1, CONV_K)])
    hp = jnp.zeros((SUBLANES, LANES), F32)
    hp = hp.at[0, :gdn_heads].set(a_log[0]).at[1, :gdn_heads].set(dt_bias[0])
    hpt = jnp.zeros((SUBLANES, LANES), F32)
    hpt = hpt.at[:gdn_heads, 0].set(a_log[0]).at[:gdn_heads, 1].set(dt_bias[0])
    oa = pl.pallas_call(
        functools.partial(_gdn_kernel, heads=gdn_heads),
        grid=(b // nsq, n_seq),
        in_specs=[pl.BlockSpec((nsq, ts, c1), lambda i, j: (i, j, 0)),
                  pl.BlockSpec((nsq, ts, LANES), lambda i, j: (i, j, 0)),
                  pl.BlockSpec((nsq, SUBLANES, ts), lambda i, j: (i, 0, j)),
                  pl.BlockSpec((CONV_K, c1), lambda i, j: (0, 0)),
                  pl.BlockSpec((CONV_K - 1, ts, ts), lambda i, j: (0, 0, 0)),
                  pl.BlockSpec((SUBLANES, LANES), lambda i, j: (0, 0)),
                  pl.BlockSpec((SUBLANES, LANES), lambda i, j: (0, 0)),
                  pl.BlockSpec((1, HEAD_DIM), lambda i, j: (0, 0))],
        out_specs=pl.BlockSpec((nsq, ts, gdn_w), lambda i, j: (i, j, 0)),
        out_shape=jax.ShapeDtypeStruct((b, s, gdn_w), BF16),
        scratch_shapes=[pltpu.VMEM((gdn_heads * nsq, HEAD_DIM, HEAD_DIM), F32),
                        pltpu.VMEM((nsq, SUBLANES, c1), F32)],
        compiler_params=pltpu.CompilerParams(dimension_semantics=("parallel", "arbitrary"),
                                             vmem_limit_bytes=VMEM_LIMIT),
        name="gdn",
    )(qkv_a.reshape(b, s, c1), ab.reshape(b, s, LANES), abt, conv_w[0].astype(F32), shifts, hp, hpt,
      row(gdn_norm_g[0]))
    oa = oa.reshape(t, gdn_w)

    nb = s // MOBA_BLOCK
    hpm = min(MOBA_HEADS_PER_STEP, moba_heads)
    assert moba_heads % hpm == 0 and SEL_LANE0 + nb <= LANES and s % ATT_TQ == 0
    slopes = 2.0 ** (-8.0 * (jnp.arange(moba_heads, dtype=F32) + 1.0) / moba_heads)
    qkv_b3 = qkv_b.reshape(b, s, 3 * moba_w)
    mg = moba_heads // hpm
    ob = pl.pallas_call(
        functools.partial(_moba_kernel, nb=nb, hp=hpm, tq=ATT_TQ),
        grid=(b, mg),
        in_specs=[pl.BlockSpec(memory_space=pltpu.SMEM),
                  pl.BlockSpec((None, s, hpm * HEAD_DIM), lambda i, h: (i, 0, h)),
                  pl.BlockSpec((None, s, hpm * HEAD_DIM), lambda i, h: (i, 0, mg + h)),
                  pl.BlockSpec((None, s, hpm * HEAD_DIM), lambda i, h: (i, 0, 2 * mg + h))],
        out_specs=pl.BlockSpec((None, s, hpm * HEAD_DIM), lambda i, h: (i, 0, h)),
        out_shape=jax.ShapeDtypeStruct((b, s, moba_w), BF16),
        scratch_shapes=[pltpu.VMEM((hpm, s, HEAD_DIM + LANES), BF16),
                        pltpu.VMEM((hpm, s, HEAD_DIM + LANES), BF16),
                        pltpu.VMEM((hpm, LANES, HEAD_DIM), F32),
                        pltpu.VMEM((2, hpm, ATT_TQ, 1), F32),
                        pltpu.VMEM((2, hpm, ATT_TQ, HEAD_DIM + LANES), F32)],
        compiler_params=pltpu.CompilerParams(
            dimension_semantics=("parallel", "parallel"), vmem_limit_bytes=VMEM_LIMIT),
        name="moba",
    )(slopes, qkv_b3, qkv_b3, qkv_b3)
    ob = ob.reshape(t, moba_w)

    w1 = w_in_c[0]
    e1 = 3 * fox_w
    e2 = e1 + fox_heads
    w1_big = jnp.concatenate([w1[:, :fox_w] * ATT_SCALE, w1[:, fox_w:e1], w1[:, e2:]],
                             axis=1).astype(BF16)
    w1_f = _pad_cols(w1[:, e1:e2], LANES).astype(BF16)
    fb = _pad_cols(row(forget_b[0]), LANES)
    ple_w = lambda i: (w_ple_gate[i].astype(BF16), w_ple_proj[i].astype(BF16))
    wg0, wp0 = ple_w(0)
    wout0 = w_out_ab[0].astype(BF16)
    x2, qkv_c, z1, cfox = pl.pallas_call(
        functools.partial(_rows_d_kernel, tiles_per_seq=s // tm),
        grid=(n_row,),
        in_specs=[_row_spec(tm, gdn_w), _row_spec(tm, moba_w), _row_spec(tm, mix_ab), _row_spec(tm, d),
                  _layer_spec(tm, PLE_DIM, 0), _const_spec(wout0.shape), _const_spec((1, d)),
                  _const_spec(wg0.shape), _const_spec(wp0.shape), _const_spec((1, d)),
                  _const_spec(w1_big.shape), _const_spec(w1_f.shape), _const_spec(fb.shape)],
        out_specs=[_row_spec(tm, d), _row_spec(tm, e1), _row_spec(tm, fox_w), _row_spec(tm, LANES)],
        out_shape=[jax.ShapeDtypeStruct((t, d), F32), jax.ShapeDtypeStruct((t, e1), BF16),
                   jax.ShapeDtypeStruct((t, fox_w), BF16), jax.ShapeDtypeStruct((t, LANES), F32)],
        scratch_shapes=[pltpu.VMEM((1, LANES), F32)],
        compiler_params=_row_params("arbitrary"),
        name="rows_d",
    )(oa, ob, z0, xf, pf, wout0, row(ple_norm_g[0]), wg0, wp0, row(norm_g[1]), w1_big, w1_f, fb)

    tq = ATT_TQ
    hpf = min(FOX_HEADS_PER_STEP, fox_heads)
    assert fox_heads % hpf == 0 and s % tq == 0
    fg = fox_heads // hpf
    qkv_c3 = qkv_c.reshape(b, s, e1)
    oc = pl.pallas_call(
        functools.partial(_fox_kernel, hp=hpf, tq=tq),
        grid=(b, fg),
        in_specs=[pl.BlockSpec((None, s, hpf * HEAD_DIM), lambda i, h: (i, 0, h)),
                  pl.BlockSpec((None, s, hpf * HEAD_DIM), lambda i, h: (i, 0, fg + h)),
                  pl.BlockSpec((None, s, hpf * HEAD_DIM), lambda i, h: (i, 0, 2 * fg + h)),
                  pl.BlockSpec((s, LANES), lambda i, h: (i, 0))],
        out_specs=pl.BlockSpec((None, s, hpf * HEAD_DIM), lambda i, h: (i, 0, h)),
        out_shape=jax.ShapeDtypeStruct((b, s, fox_w), BF16),
        scratch_shapes=[pltpu.VMEM((hpf, s, HEAD_DIM + LANES), BF16),
                        pltpu.VMEM((hpf, s, HEAD_DIM + LANES), BF16),
                        pltpu.VMEM((2, hpf, tq, 1), F32),
                        pltpu.VMEM((2, hpf, tq, HEAD_DIM + LANES), F32)],
        compiler_params=pltpu.CompilerParams(
            dimension_semantics=("parallel", "parallel"), vmem_limit_bytes=VMEM_LIMIT),
        name="fox",
    )(qkv_c3, qkv_c3, qkv_c3, cfox)
    oc = oc.reshape(t, fox_w)

    wg1, wp1 = ple_w(1)
    wout1 = w_out_c[0].astype(BF16)
    out = pl.pallas_call(
        _rows_f_kernel,
        grid=(n_row,),
        in_specs=[_row_spec(tm, fox_w), _row_spec(tm, fox_w), _row_spec(tm, d), _layer_spec(tm, PLE_DIM, 1),
                  _const_spec(wout1.shape), _const_spec((1, d)), _const_spec(wg1.shape),
                  _const_spec(wp1.shape), _const_spec((1, d))],
        out_specs=_row_spec(tm, d),
        out_shape=jax.ShapeDtypeStruct((t, d), F32),
        compiler_params=_row_params("parallel"),
        name="rows_f",
    )(oc, z1, x2, pf, wout1, row(ple_norm_g[1]), wg1, wp1, row(final_g))
    return out.reshape(b, s, d)
```

```python
import functools

import jax
import jax.numpy as jnp
from jax import lax
from jax.experimental import pallas as pl
from jax.experimental.pallas import tpu as pltpu

F32 = jnp.float32
BF16 = jnp.bfloat16

HEAD_DIM = 128
PLE_DIM = 256
CONV_K = 4
GDN_CHUNK = 64
MOBA_BLOCK = 256
MOBA_TOPK = 3
RMS_EPS = 1e-6
NEG = -(2.0 ** 100)
LANES = 128
SUBLANES = 8
SMALL_ROWS = 16
VMEM_LIMIT = 56 * 1024 * 1024

LOG2E = 1.4426950408889634
ATT_SCALE = HEAD_DIM ** -0.5 * LOG2E

ROW_TILE = 512
PROJ_CHUNK = 512
GDN_TILE = 256
GDN_SEQS_PER_STEP = 4
ATT_TQ = 1024
MOBA_HEADS_PER_STEP = 1
FOX_HEADS_PER_STEP = 1
BIAS_PARTS = 3
SEL_LANE0 = 8


def _sigmoid(x):
    return 1.0 / (1.0 + jnp.exp(-x))


def _silu(x):
    return x * _sigmoid(x)


def _softplus(x):
    return jnp.maximum(x, 0.0) + jnp.log1p(jnp.exp(-jnp.abs(x)))


def _log_sigmoid(x):
    return jnp.minimum(x, 0.0) - jnp.log1p(jnp.exp(-jnp.abs(x)))


def _rms(xf, g):
    return xf * lax.rsqrt(jnp.mean(xf * xf, axis=-1, keepdims=True) + RMS_EPS) * g


def _dot(a, b):
    return jnp.dot(a, b, preferred_element_type=F32)


def _dot_nt(a, b):
    return lax.dot_general(a, b, (((1,), (1,)), ((), ())), preferred_element_type=F32)


def _seg_cumsum(v, seg, axis):
    pos = lax.broadcasted_iota(jnp.int32, v.shape, axis) & (seg - 1)
    s = 1
    while s < seg:
        v = v + jnp.where(pos >= s, pltpu.roll(v, s, axis=axis), 0.0)
        s *= 2
    return v


def _mix_ple(o_parts, z, x, p, w_out_ref, gple, w_gate_ref, w_proj_ref):
    gz = _silu(z.astype(F32))
    acc = x
    off = 0
    for o in o_parts:
        w = o.shape[1]
        y = (o.astype(F32) * gz[:, off:off + w]).astype(BF16)
        acc = acc + _dot(y, w_out_ref[off:off + w, :])
        off += w
    hn = _rms(acc, gple).astype(BF16)
    gate = _sigmoid(_dot(hn, w_gate_ref[...]))
    pp = _dot(p.astype(BF16), w_proj_ref[...])
    return acc + gate * pp


def _project(hn, w_ref, out_refs):
    off = 0
    for o_ref in out_refs:
        n = o_ref.shape[1]
        for c0 in range(0, n, PROJ_CHUNK):
            c1 = min(n, c0 + PROJ_CHUNK)
            o_ref[:, c0:c1] = _dot(hn, w_ref[:, off + c0:off + c1]).astype(o_ref.dtype)
        off += n


def _rows_a_kernel(x_ref, g_ref, w_ref, ws_ref, wst_ref,
                   qkva_ref, qkvb_ref, z_ref, ab_ref, abt_ref):
    hn = _rms(x_ref[...], g_ref[...]).astype(BF16)
    _project(hn, w_ref, (qkva_ref, qkvb_ref, z_ref))
    ab_ref[...] = _dot(hn, ws_ref[...])
    abt_ref[...] = _dot_nt(wst_ref[...], hn)[:SUBLANES]


def _rows_d_kernel(oa_ref, ob_ref, z_ref, x_ref, p_ref, wout_ref, gple_ref, wgate_ref, wproj_ref,
                   g_ref, w_ref, wf_ref, fb_ref,
                   x2_ref, qkv_ref, z1_ref, c_ref, carry_ref, *, tiles_per_seq):
    x2 = _mix_ple((oa_ref[...], ob_ref[...]), z_ref[...], x_ref[...], p_ref[...],
                  wout_ref, gple_ref[...], wgate_ref, wproj_ref)
    x2_ref[...] = x2
    hn = _rms(x2, g_ref[...]).astype(BF16)
    _project(hn, w_ref, (qkv_ref, z1_ref))

    @pl.when(pl.program_id(0) % tiles_per_seq == 0)
    def _():
        carry_ref[...] = jnp.zeros_like(carry_ref)

    log_f = _log_sigmoid(_dot(hn, wf_ref[...]) + fb_ref[...])
    tm = log_f.shape[0]
    c = _seg_cumsum(log_f, tm, 0) + carry_ref[...]
    c_ref[...] = c
    carry_ref[...] = c[tm - 1:tm, :]


def _rows_f_kernel(o_ref, z_ref, x_ref, p_ref, wout_ref, gple_ref, wgate_ref, wproj_ref,
                   g_ref, out_ref):
    x2 = _mix_ple((o_ref[...],), z_ref[...], x_ref[...], p_ref[...],
                  wout_ref, gple_ref[...], wgate_ref, wproj_ref)
    out_ref[...] = _rms(x2, g_ref[...])


def _row_spec(tm, n):
    return pl.BlockSpec((tm, n), lambda i: (i, 0))


def _layer_spec(tm, n, layer):
    return pl.BlockSpec((None, tm, n), lambda i: (layer, i, 0))


def _const_spec(shape):
    return pl.BlockSpec(shape, lambda i: (0,) * len(shape), pipeline_mode=pl.Buffered(1))


def _row_params(semantics):
    return pltpu.CompilerParams(dimension_semantics=(semantics,), vmem_limit_bytes=VMEM_LIMIT)


def _bdot(a, b):
    return lax.dot_general(a, b, (((2,), (1,)), ((0,), (0,))), preferred_element_type=F32)


def _bdot_nt(a, b):
    return lax.dot_general(a, b, (((2,), (2,)), ((0,), (0,))), preferred_element_type=F32)


def _bdot_tn(a, b):
    return lax.dot_general(a, b, (((1,), (1,)), ((0,), (0,))), preferred_element_type=F32)


def _inv_unit_lower(a):
    c = a.shape[-1]
    eye = (lax.broadcasted_iota(jnp.int32, (c, c), 0)
           == lax.broadcasted_iota(jnp.int32, (c, c), 1)).astype(F32)
    t = eye - a
    p = a
    k = 2
    while k < c:
        pb = p.astype(BF16)
        p = _bdot(pb, pb)
        t = t + _bdot(t.astype(BF16), p.astype(BF16))
        k *= 2
    return t


def _gdn_kernel(qkv_ref, ab_ref, abt_ref, cw_ref, shift_ref, hp_ref, hpt_ref, gn_ref,
                o_ref, state_ref, tail_ref, *, heads):
    nseq, ts = qkv_ref.shape[0], qkv_ref.shape[1]
    width = heads * HEAD_DIM
    ck = GDN_CHUNK
    nc = ts // ck

    @pl.when(pl.program_id(1) == 0)
    def _():
        state_ref[...] = jnp.zeros_like(state_ref)
        tail_ref[...] = jnp.zeros_like(tail_ref)

    cw = cw_ref[...]
    hp = hp_ref[...]
    hpt = hpt_ref[...]
    xcs, beta_cols, gcum_cols, gcum_rows = [], [], [], []
    for sq in range(nseq):
        ub = qkv_ref[sq]
        u = ub.astype(F32)
        prev = tail_ref[sq]
        tail_ref[sq] = u[ts - SUBLANES:, :]
        row8 = lax.broadcasted_iota(jnp.int32, prev.shape, 0)
        acc = u * cw[CONV_K - 1:CONV_K, :]
        head = jnp.zeros(prev.shape, F32)
        for s in range(1, CONV_K):
            w_s = cw[CONV_K - 1 - s:CONV_K - s, :]
            acc = acc + _dot(shift_ref[s - 1], ub) * w_s
            head = head + jnp.where(row8 < s, pltpu.roll(prev, s, axis=0), 0.0) * w_s
        acc = jnp.concatenate([acc[:SUBLANES] + head, acc[SUBLANES:]], axis=0)
        xcs.append(_silu(acc))
        ab = ab_ref[sq]
        g_col = -jnp.exp(hp[0:1, :]) * _softplus(ab + hp[1:2, :])
        beta_cols.append(_sigmoid(ab))
        gcum_cols.append(_seg_cumsum(g_col, ck, 0))
        g_row = -jnp.exp(hpt[:, 0:1]) * _softplus(abt_ref[sq] + hpt[:, 1:2])
        gcum_rows.append(_seg_cumsum(g_row, ck, 1))

    groups = [(h, sq) for h in range(heads) for sq in range(nseq)]
    ng = len(groups)

    def chunks(col0):
        return jnp.concatenate(
            [xcs[sq][:, col0 + h * HEAD_DIM:col0 + (h + 1) * HEAD_DIM].reshape(nc, ck, HEAD_DIM)
             for h, sq in groups], axis=0)

    def col_chunks(cols, lane0):
        return jnp.concatenate(
            [cols[sq][:, lane0 + h:lane0 + h + 1].reshape(nc, ck, 1) for h, sq in groups], axis=0)

    q3, k3, v3 = chunks(0), chunks(width), chunks(2 * width)
    q3 = q3 * lax.rsqrt(jnp.sum(q3 * q3, axis=-1, keepdims=True) + RMS_EPS) * (HEAD_DIM ** -0.5)
    k3 = k3 * lax.rsqrt(jnp.sum(k3 * k3, axis=-1, keepdims=True) + RMS_EPS)
    beta3 = col_chunks(beta_cols, heads)
    gc3 = col_chunks(gcum_cols, 0)
    gr3 = jnp.stack([gcum_rows[sq][h:h + 1, c * ck:(c + 1) * ck]
                     for h, sq in groups for c in range(nc)], axis=0)

    ri = lax.broadcasted_iota(jnp.int32, (ck, ck), 0)
    ci = lax.broadcasted_iota(jnp.int32, (ck, ck), 1)
    incl = ri >= ci
    strict = ri > ci
    decay = jnp.where(incl, jnp.exp(jnp.where(incl, gc3 - gr3, 0.0)), 0.0)
    kb3 = k3 * beta3
    k3b = k3.astype(BF16)
    a_mat = jnp.where(strict, _bdot_nt(kb3.astype(BF16), k3b) * decay, 0.0)
    t_inv = _inv_unit_lower(a_mat)
    exg = jnp.exp(gc3)
    rhs = jnp.concatenate([v3 * beta3, kb3 * exg], axis=2).astype(BF16)
    sol = _bdot(t_inv.astype(BF16), rhs)
    u3, w3 = sol[:, :, :HEAD_DIM], sol[:, :, HEAD_DIM:]
    qk3 = jnp.where(incl, _bdot_nt(q3.astype(BF16), k3b) * decay, 0.0).astype(BF16)
    g_last = gc3[:, ck - 1:ck, :]
    wq3 = jnp.concatenate([w3, q3 * exg], axis=1).astype(BF16)
    kdec3 = (k3 * jnp.exp(g_last - gc3)).astype(BF16)
    gl3 = jnp.exp(g_last)

    by_group = lambda a: a.reshape((ng, nc) + a.shape[1:])
    u4, qk4, wq4, kdec4, gl4 = (by_group(a) for a in (u3, qk3, wq3, kdec3, gl3))
    gn = gn_ref[...]
    state = state_ref[...]
    for c in range(nc):
        ws = _bdot(wq4[:, c], state.astype(BF16))
        v_new = (u4[:, c] - ws[:, :ck]).astype(BF16)
        o = ws[:, ck:] + _bdot(qk4[:, c], v_new)
        state = state * gl4[:, c] + _bdot_tn(kdec4[:, c], v_new)
        o = _rms(o, gn).astype(o_ref.dtype)
        for gi, (h, sq) in enumerate(groups):
            o_ref[sq, c * ck:(c + 1) * ck, h * HEAD_DIM:(h + 1) * HEAD_DIM] = o[gi]
    state_ref[...] = state


def _split_bias(x):
    parts = []
    for _ in range(BIAS_PARTS):
        piece = x.astype(BF16).astype(F32)
        parts.append(piece)
        x = x - piece
    return parts


def _key_ext(bias, extra=None):
    lane = lax.broadcasted_iota(jnp.int32, (bias.shape[0], LANES), 1)
    ext = jnp.zeros((bias.shape[0], LANES), F32) if extra is None else extra
    for i, piece in enumerate(_split_bias(bias)):
        ext = jnp.where(lane == i, piece, ext)
    return ext.astype(BF16)


def _ones_lane0(n):
    lane = lax.broadcasted_iota(jnp.int32, (n, LANES), 1)
    return jnp.where(lane == 0, 1.0, 0.0).astype(BF16)


def _logits(qxs, kx_ref, tk, k0):
    return [_dot_nt(qx, kx_ref[h, pl.ds(k0, tk), :]) for h, qx in enumerate(qxs)]


def _absorb(ss, vx_ref, m_ref, acc_ref, k0, mask, rows):
    width = ss[0].shape[1]
    if mask is not None:
        ss = [jnp.where(mask, s, NEG) for s in ss]
    for h, s in enumerate(ss):
        m_old = m_ref[h, rows, :]
        m_new = jnp.maximum(m_old, jnp.max(s, axis=1, keepdims=True))
        pv = _dot(jnp.exp2(s - m_new).astype(BF16), vx_ref[h, pl.ds(k0, width), :])
        acc_ref[h, rows, :] = jnp.exp2(m_old - m_new) * acc_ref[h, rows, :] + pv
        m_ref[h, rows, :] = m_new


def _softmax_out(acc_ref, h, dtype):
    accx = acc_ref[h]
    return (accx[:, :HEAD_DIM] / accx[:, HEAD_DIM:HEAD_DIM + 1]).astype(dtype)


def _head_cols(hh):
    return slice(hh * HEAD_DIM, (hh + 1) * HEAD_DIM)


def _causal_triangle(make_qxs, kx_ref, vx_ref, m_ref, acc_ref, o_ref, tq, hp):
    s_len = o_ref.shape[0]
    nq = s_len // tq
    every = slice(0, tq)
    half = tq // 2

    def causal(rows, width, row0):
        return (lax.broadcasted_iota(jnp.int32, (rows, width), 0) + row0
                >= lax.broadcasted_iota(jnp.int32, (rows, width), 1))

    m_ref[...] = jnp.full(m_ref.shape, NEG, F32)
    acc_ref[...] = jnp.zeros(acc_ref.shape, F32)
    rows = [slice(qi * tq, (qi + 1) * tq) for qi in range(nq)]
    qx_all = [make_qxs(qi, rows[qi]) for qi in range(nq)]
    for t in range(nq):
        k0 = t * tq
        m_q, acc_q = m_ref.at[t], acc_ref.at[t]
        _absorb(_logits([qx[:half] for qx in qx_all[t]], kx_ref, half, k0), vx_ref, m_q, acc_q, k0,
                causal(half, half, 0), slice(0, half))
        _absorb(_logits([qx[half:] for qx in qx_all[t]], kx_ref, tq, k0), vx_ref, m_q, acc_q, k0,
                causal(half, tq, half), slice(half, tq))
        for hh in range(hp):
            o_ref[rows[t], _head_cols(hh)] = _softmax_out(acc_q, hh, o_ref.dtype)
        for qi in range(t + 1, nq):
            _absorb(_logits(qx_all[qi], kx_ref, tq, k0), vx_ref, m_ref.at[qi], acc_ref.at[qi], k0, None,
                    every)


def _moba_kernel(slopes_ref, q_ref, k_ref, v_ref, o_ref, kx_ref, vx_ref, kmean_ref, m_ref, acc_ref,
                 *, nb, hp, tq):
    blk = MOBA_BLOCK
    s_len = k_ref.shape[0]
    hg = pl.program_id(1)
    nbp = -(-nb // SUBLANES) * SUBLANES

    pos = lax.broadcasted_iota(jnp.int32, (s_len, 1), 0)
    lane_s = lax.broadcasted_iota(jnp.int32, (s_len, LANES), 1)
    onehot = jnp.where(lane_s - SEL_LANE0 == lax.broadcasted_iota(jnp.int32, (s_len, LANES), 0) // blk,
                       1.0, 0.0)
    ones = _ones_lane0(s_len)
    kmean_ref[...] = jnp.zeros_like(kmean_ref)
    for hh in range(hp):
        slope2 = slopes_ref[hg * hp + hh] * LOG2E
        kx_ref[hh, :, :HEAD_DIM] = k_ref[:, _head_cols(hh)]
        kx_ref[hh, :, HEAD_DIM:] = _key_ext(slope2 * pos.astype(F32), onehot)
        vx_ref[hh, :, :HEAD_DIM] = v_ref[:, _head_cols(hh)]
        vx_ref[hh, :, HEAD_DIM:] = ones
        for j in range(nb):
            kmean_ref[hh, j:j + 1, :] = jnp.mean(
                k_ref[j * blk:(j + 1) * blk, _head_cols(hh)].astype(F32), axis=0, keepdims=True)

    blk_id = lax.broadcasted_iota(jnp.int32, (nbp, tq), 0)
    lane = lax.broadcasted_iota(jnp.int32, (tq, LANES), 1)

    def make_qxs(qi, rows):
        q_blk = qi * (tq // blk) + lax.broadcasted_iota(jnp.int32, (nbp, tq), 1) // blk
        past = blk_id < q_blk
        qxs = []
        for hh in range(hp):
            q = q_ref[rows, _head_cols(hh)]
            gate = sum(_dot_nt(piece.astype(BF16), q) for piece in _split_bias(kmean_ref[hh, :nbp, :]))
            gm = jnp.where(past, gate, NEG)
            rank = jnp.zeros(gm.shape, jnp.int32)
            for j in range(nb):
                cj = gm[j:j + 1, :]
                beats = (cj > gm) | ((cj == gm) & (blk_id > j))
                rank = rank + beats.astype(jnp.int32)
            sel_t = jnp.where(((rank < MOBA_TOPK) & past) | (blk_id == q_blk), 0.0, NEG)
            sel_t = jnp.concatenate([jnp.zeros((SEL_LANE0, tq), F32), sel_t,
                                     jnp.zeros((LANES - SEL_LANE0 - nbp, tq), F32)], axis=0)
            qext = jnp.where(lane < BIAS_PARTS, 1.0, sel_t.T).astype(BF16)
            qxs.append(jnp.concatenate([q, qext], axis=1))
        return qxs

    _causal_triangle(make_qxs, kx_ref, vx_ref, m_ref, acc_ref, o_ref, tq, hp)


def _fox_kernel(q_ref, k_ref, v_ref, c_ref, o_ref, kx_ref, vx_ref, m_ref, acc_ref, *, hp, tq):
    s_len = k_ref.shape[0]
    hg = pl.program_id(1)
    lane_s = lax.broadcasted_iota(jnp.int32, (s_len, LANES), 1)
    ones = _ones_lane0(s_len)
    for hh in range(hp):
        c_col = jnp.sum(jnp.where(lane_s == hg * hp + hh, c_ref[...], 0.0), axis=1, keepdims=True)
        kx_ref[hh, :, :HEAD_DIM] = k_ref[:, _head_cols(hh)]
        kx_ref[hh, :, HEAD_DIM:] = _key_ext(-LOG2E * c_col)
        vx_ref[hh, :, :HEAD_DIM] = v_ref[:, _head_cols(hh)]
        vx_ref[hh, :, HEAD_DIM:] = ones

    lane = lax.broadcasted_iota(jnp.int32, (tq, LANES), 1)
    qext = jnp.where(lane < BIAS_PARTS, 1.0, 0.0).astype(BF16)

    def make_qxs(qi, rows):
        return [jnp.concatenate([q_ref[rows, _head_cols(hh)], qext], axis=1) for hh in range(hp)]

    _causal_triangle(make_qxs, kx_ref, vx_ref, m_ref, acc_ref, o_ref, tq, hp)


def _pad_cols(w, n):
    return jnp.pad(w, ((0, 0), (0, n - w.shape[1])))


def kernel(x, p, norm_g, w_in_ab, conv_w, a_log, dt_bias, gdn_norm_g, w_out_ab, w_in_c, forget_b,
           w_out_c, ple_norm_g, w_ple_gate, w_ple_proj, final_g):
    b, s, d = x.shape
    t = b * s
    gdn_heads = a_log.shape[1]
    gdn_w = gdn_heads * HEAD_DIM
    moba_w = (w_out_ab.shape[1] - gdn_w)
    moba_heads = moba_w // HEAD_DIM
    fox_heads = forget_b.shape[1]
    fox_w = fox_heads * HEAD_DIM
    mix_ab = gdn_w + moba_w
    assert s % MOBA_BLOCK == 0 and s % ROW_TILE == 0 and s % GDN_TILE == 0
    assert 2 * gdn_heads <= SUBLANES and fox_heads <= LANES
    tm = ROW_TILE
    n_row = t // tm

    xf = x.reshape(t, d)
    pf = p.reshape(p.shape[0], t, PLE_DIM)
    row = lambda v: v.reshape(1, -1).astype(F32)

    w0 = w_in_ab[0]
    c1 = 3 * gdn_w
    c2 = c1 + 2 * gdn_heads
    w0_big = jnp.concatenate([w0[:, :c1], w0[:, c2:c2 + moba_w] * ATT_SCALE, w0[:, c2 + moba_w:]],
                             axis=1).astype(BF16)
    w0_small = w0[:, c1:c2]
    w0_s = _pad_cols(w0_small, LANES).astype(BF16)
    w0_st = jnp.pad(w0_small.T, ((0, SMALL_ROWS - 2 * gdn_heads), (0, 0))).astype(BF16)

    qkv_a, qkv_b, z0, ab, abt = pl.pallas_call(
        _rows_a_kernel,
        grid=(n_row,),
        in_specs=[_row_spec(tm, d), _const_spec((1, d)), _const_spec(w0_big.shape),
                  _const_spec(w0_s.shape), _const_spec(w0_st.shape)],
        out_specs=[_row_spec(tm, c1), _row_spec(tm, 3 * moba_w), _row_spec(tm, mix_ab),
                   _row_spec(tm, LANES),
                   pl.BlockSpec((None, SUBLANES, tm), lambda i: (i // (s // tm), 0, i % (s // tm)))],
        out_shape=[jax.ShapeDtypeStruct((t, c1), BF16), jax.ShapeDtypeStruct((t, 3 * moba_w), BF16),
                   jax.ShapeDtypeStruct((t, mix_ab), BF16), jax.ShapeDtypeStruct((t, LANES), F32),
                   jax.ShapeDtypeStruct((b, SUBLANES, s), F32)],
        compiler_params=_row_params("parallel"),
        name="rows_a",
    )(xf, row(norm_g[0]), w0_big, w0_s, w0_st)

    ts = GDN_TILE
    n_seq = s // ts
    nsq = GDN_SEQS_PER_STEP if b % GDN_SEQS_PER_STEP == 0 else 1
    tpos = jnp.arange(ts)
    shifts = jnp.stack([(tpos[:, None] - tpos[None, :] == sh).astype(BF16) for sh in range(1, CONV_K)])
    hp = jnp.zeros((SUBLANES, LANES), F32)
    hp = hp.at[0, :gdn_heads].set(a_log[0]).at[1, :gdn_heads].set(dt_bias[0])
    hpt = jnp.zeros((SUBLANES, LANES), F32)
    hpt = hpt.at[:gdn_heads, 0].set(a_log[0]).at[:gdn_heads, 1].set(dt_bias[0])
    oa = pl.pallas_call(
        functools.partial(_gdn_kernel, heads=gdn_heads),
        grid=(b // nsq, n_seq),
        in_specs=[pl.BlockSpec((nsq, ts, c1), lambda i, j: (i, j, 0)),
                  pl.BlockSpec((nsq, ts, LANES), lambda i, j: (i, j, 0)),
                  pl.BlockSpec((nsq, SUBLANES, ts), lambda i, j: (i, 0, j)),
                  pl.BlockSpec((CONV_K, c1), lambda i, j: (0, 0)),
                  pl.BlockSpec((CONV_K - 1, ts, ts), lambda i, j: (0, 0, 0)),
                  pl.BlockSpec((SUBLANES, LANES), lambda i, j: (0, 0)),
                  pl.BlockSpec((SUBLANES, LANES), lambda i, j: (0, 0)),
                  pl.BlockSpec((1, HEAD_DIM), lambda i, j: (0, 0))],
        out_specs=pl.BlockSpec((nsq, ts, gdn_w), lambda i, j: (i, j, 0)),
        out_shape=jax.ShapeDtypeStruct((b, s, gdn_w), BF16),
        scratch_shapes=[pltpu.VMEM((gdn_heads * nsq, HEAD_DIM, HEAD_DIM), F32),
                        pltpu.VMEM((nsq, SUBLANES, c1), F32)],
        compiler_params=pltpu.CompilerParams(dimension_semantics=("parallel", "arbitrary"),
                                             vmem_limit_bytes=VMEM_LIMIT),
        name="gdn",
    )(qkv_a.reshape(b, s, c1), ab.reshape(b, s, LANES), abt, conv_w[0].astype(F32), shifts, hp, hpt,
      row(gdn_norm_g[0]))
    oa = oa.reshape(t, gdn_w)

    nb = s // MOBA_BLOCK
    hpm = min(MOBA_HEADS_PER_STEP, moba_heads)
    assert moba_heads % hpm == 0 and SEL_LANE0 + nb <= LANES and s % ATT_TQ == 0
    slopes = 2.0 ** (-8.0 * (jnp.arange(moba_heads, dtype=F32) + 1.0) / moba_heads)
    qkv_b3 = qkv_b.reshape(b, s, 3 * moba_w)
    mg = moba_heads // hpm
    ob = pl.pallas_call(
        functools.partial(_moba_kernel, nb=nb, hp=hpm, tq=ATT_TQ),
        grid=(b, mg),
        in_specs=[pl.BlockSpec(memory_space=pltpu.SMEM),
                  pl.BlockSpec((None, s, hpm * HEAD_DIM), lambda i, h: (i, 0, h)),
                  pl.BlockSpec((None, s, hpm * HEAD_DIM), lambda i, h: (i, 0, mg + h)),
                  pl.BlockSpec((None, s, hpm * HEAD_DIM), lambda i, h: (i, 0, 2 * mg + h))],
        out_specs=pl.BlockSpec((None, s, hpm * HEAD_DIM), lambda i, h: (i, 0, h)),
        out_shape=jax.ShapeDtypeStruct((b, s, moba_w), BF16),
        scratch_shapes=[pltpu.VMEM((hpm, s, HEAD_DIM + LANES), BF16),
                        pltpu.VMEM((hpm, s, HEAD_DIM + LANES), BF16),
                        pltpu.VMEM((hpm, LANES, HEAD_DIM), F32),
                        pltpu.VMEM((s // ATT_TQ, hpm, ATT_TQ, 1), F32),
                        pltpu.VMEM((s // ATT_TQ, hpm, ATT_TQ, HEAD_DIM + LANES), F32)],
        compiler_params=pltpu.CompilerParams(
            dimension_semantics=("parallel", "parallel"), vmem_limit_bytes=VMEM_LIMIT),
        name="moba",
    )(slopes, qkv_b3, qkv_b3, qkv_b3)
    ob = ob.reshape(t, moba_w)

    w1 = w_in_c[0]
    e1 = 3 * fox_w
    e2 = e1 + fox_heads
    w1_big = jnp.concatenate([w1[:, :fox_w] * ATT_SCALE, w1[:, fox_w:e1], w1[:, e2:]],
                             axis=1).astype(BF16)
    w1_f = _pad_cols(w1[:, e1:e2], LANES).astype(BF16)
    fb = _pad_cols(row(forget_b[0]), LANES)
    ple_w = lambda i: (w_ple_gate[i].astype(BF16), w_ple_proj[i].astype(BF16))
    wg0, wp0 = ple_w(0)
    wout0 = w_out_ab[0].astype(BF16)
    x2, qkv_c, z1, cfox = pl.pallas_call(
        functools.partial(_rows_d_kernel, tiles_per_seq=s // tm),
        grid=(n_row,),
        in_specs=[_row_spec(tm, gdn_w), _row_spec(tm, moba_w), _row_spec(tm, mix_ab), _row_spec(tm, d),
                  _layer_spec(tm, PLE_DIM, 0), _const_spec(wout0.shape), _const_spec((1, d)),
                  _const_spec(wg0.shape), _const_spec(wp0.shape), _const_spec((1, d)),
                  _const_spec(w1_big.shape), _const_spec(w1_f.shape), _const_spec(fb.shape)],
        out_specs=[_row_spec(tm, d), _row_spec(tm, e1), _row_spec(tm, fox_w), _row_spec(tm, LANES)],
        out_shape=[jax.ShapeDtypeStruct((t, d), F32), jax.ShapeDtypeStruct((t, e1), BF16),
                   jax.ShapeDtypeStruct((t, fox_w), BF16), jax.ShapeDtypeStruct((t, LANES), F32)],
        scratch_shapes=[pltpu.VMEM((1, LANES), F32)],
        compiler_params=_row_params("arbitrary"),
        name="rows_d",
    )(oa, ob, z0, xf, pf, wout0, row(ple_norm_g[0]), wg0, wp0, row(norm_g[1]), w1_big, w1_f, fb)

    tq = ATT_TQ
    hpf = min(FOX_HEADS_PER_STEP, fox_heads)
    assert fox_heads % hpf == 0 and s % tq == 0
    fg = fox_heads // hpf
    qkv_c3 = qkv_c.reshape(b, s, e1)
    oc = pl.pallas_call(
        functools.partial(_fox_kernel, hp=hpf, tq=tq),
        grid=(b, fg),
        in_specs=[pl.BlockSpec((None, s, hpf * HEAD_DIM), lambda i, h: (i, 0, h)),
                  pl.BlockSpec((None, s, hpf * HEAD_DIM), lambda i, h: (i, 0, fg + h)),
                  pl.BlockSpec((None, s, hpf * HEAD_DIM), lambda i, h: (i, 0, 2 * fg + h)),
                  pl.BlockSpec((s, LANES), lambda i, h: (i, 0))],
        out_specs=pl.BlockSpec((None, s, hpf * HEAD_DIM), lambda i, h: (i, 0, h)),
        out_shape=jax.ShapeDtypeStruct((b, s, fox_w), BF16),
        scratch_shapes=[pltpu.VMEM((hpf, s, HEAD_DIM + LANES), BF16),
                        pltpu.VMEM((hpf, s, HEAD_DIM + LANES), BF16),
                        pltpu.VMEM((s // tq, hpf, tq, 1), F32),
                        pltpu.VMEM((s // tq, hpf, tq, HEAD_DIM + LANES), F32)],
        compiler_params=pltpu.CompilerParams(
            dimension_semantics=("parallel", "parallel"), vmem_limit_bytes=VMEM_LIMIT),
        name="fox",
    )(qkv_c3, qkv_c3, qkv_c3, cfox)
    oc = oc.reshape(t, fox_w)

    wg1, wp1 = ple_w(1)
    wout1 = w_out_c[0].astype(BF16)
    out = pl.pallas_call(
        _rows_f_kernel,
        grid=(n_row,),
        in_specs=[_row_spec(tm, fox_w), _row_spec(tm, fox_w), _row_spec(tm, d), _layer_spec(tm, PLE_DIM, 1),
                  _const_spec(wout1.shape), _const_spec((1, d)), _const_spec(wg1.shape),
                  _const_spec(wp1.shape), _const_spec((1, d))],
        out_specs=_row_spec(tm, d),
        out_shape=jax.ShapeDtypeStruct((t, d), F32),
        compiler_params=_row_params("parallel"),
        name="rows_f",
    )(oc, z1, x2, pf, wout1, row(ple_norm_g[1]), wg1, wp1, row(final_g))
    return out.reshape(b, s, d)
```

```python
import functools

import jax
import jax.numpy as jnp
from jax import lax
from jax.experimental import pallas as pl
from jax.experimental.pallas import tpu as pltpu

F32 = jnp.float32
BF16 = jnp.bfloat16

HEAD_DIM = 128
PLE_DIM = 256
CONV_K = 4
GDN_CHUNK = 64
MOBA_BLOCK = 256
MOBA_TOPK = 3
RMS_EPS = 1e-6
NEG = -(2.0 ** 100)
LANES = 128
SUBLANES = 8
SMALL_ROWS = 16
VMEM_LIMIT = 56 * 1024 * 1024

LOG2E = 1.4426950408889634
ATT_SCALE = HEAD_DIM ** -0.5 * LOG2E

ROW_TILE = 512
PROJ_CHUNK = 512
GDN_TILE = 256
GDN_SEQS_PER_STEP = 4
ATT_TQ = 1024
MOBA_HEADS_PER_STEP = 1
FOX_HEADS_PER_STEP = 1
BIAS_PARTS = 3
SEL_LANE0 = 8


def _sigmoid(x):
    return 1.0 / (1.0 + jnp.exp(-x))


def _silu(x):
    return x * _sigmoid(x)


def _softplus(x):
    return jnp.maximum(x, 0.0) + jnp.log1p(jnp.exp(-jnp.abs(x)))


def _log_sigmoid(x):
    return jnp.minimum(x, 0.0) - jnp.log1p(jnp.exp(-jnp.abs(x)))


def _rms(xf, g):
    return xf * lax.rsqrt(jnp.mean(xf * xf, axis=-1, keepdims=True) + RMS_EPS) * g


def _dot(a, b):
    return jnp.dot(a, b, preferred_element_type=F32)


def _dot_nt(a, b):
    return lax.dot_general(a, b, (((1,), (1,)), ((), ())), preferred_element_type=F32)


def _seg_cumsum(v, seg, axis):
    pos = lax.broadcasted_iota(jnp.int32, v.shape, axis) & (seg - 1)
    s = 1
    while s < seg:
        v = v + jnp.where(pos >= s, pltpu.roll(v, s, axis=axis), 0.0)
        s *= 2
    return v


def _mix_ple(o_parts, z, x, p, w_out_ref, gple, w_gate_ref, w_proj_ref):
    gz = _silu(z.astype(F32))
    acc = x
    off = 0
    for o in o_parts:
        w = o.shape[1]
        y = (o.astype(F32) * gz[:, off:off + w]).astype(BF16)
        acc = acc + _dot(y, w_out_ref[off:off + w, :])
        off += w
    hn = _rms(acc, gple).astype(BF16)
    gate = _sigmoid(_dot(hn, w_gate_ref[...]))
    pp = _dot(p.astype(BF16), w_proj_ref[...])
    return acc + gate * pp


def _project(hn, w_ref, out_refs):
    off = 0
    for o_ref in out_refs:
        n = o_ref.shape[1]
        for c0 in range(0, n, PROJ_CHUNK):
            c1 = min(n, c0 + PROJ_CHUNK)
            o_ref[:, c0:c1] = _dot(hn, w_ref[:, off + c0:off + c1]).astype(o_ref.dtype)
        off += n


def _rows_a_kernel(x_ref, g_ref, w_ref, ws_ref, wst_ref,
                   qkva_ref, qkvb_ref, z_ref, ab_ref, abt_ref):
    hn = _rms(x_ref[...], g_ref[...]).astype(BF16)
    _project(hn, w_ref, (qkva_ref, qkvb_ref, z_ref))
    ab_ref[...] = _dot(hn, ws_ref[...])
    abt_ref[...] = _dot_nt(wst_ref[...], hn)[:SUBLANES]


def _rows_d_kernel(oa_ref, ob_ref, z_ref, x_ref, p_ref, wout_ref, gple_ref, wgate_ref, wproj_ref,
                   g_ref, w_ref, wf_ref, fb_ref,
                   x2_ref, qkv_ref, z1_ref, c_ref, carry_ref, *, tiles_per_seq):
    x2 = _mix_ple((oa_ref[...], ob_ref[...]), z_ref[...], x_ref[...], p_ref[...],
                  wout_ref, gple_ref[...], wgate_ref, wproj_ref)
    x2_ref[...] = x2
    hn = _rms(x2, g_ref[...]).astype(BF16)
    _project(hn, w_ref, (qkv_ref, z1_ref))

    @pl.when(pl.program_id(0) % tiles_per_seq == 0)
    def _():
        carry_ref[...] = jnp.zeros_like(carry_ref)

    log_f = _log_sigmoid(_dot(hn, wf_ref[...]) + fb_ref[...])
    tm = log_f.shape[0]
    c = _seg_cumsum(log_f, tm, 0) + carry_ref[...]
    c_ref[...] = c
    carry_ref[...] = c[tm - 1:tm, :]


def _rows_f_kernel(o_ref, z_ref, x_ref, p_ref, wout_ref, gple_ref, wgate_ref, wproj_ref,
                   g_ref, out_ref):
    x2 = _mix_ple((o_ref[...],), z_ref[...], x_ref[...], p_ref[...],
                  wout_ref, gple_ref[...], wgate_ref, wproj_ref)
    out_ref[...] = _rms(x2, g_ref[...])


def _row_spec(tm, n):
    return pl.BlockSpec((tm, n), lambda i: (i, 0))


def _layer_spec(tm, n, layer):
    return pl.BlockSpec((None, tm, n), lambda i: (layer, i, 0))


def _const_spec(shape):
    return pl.BlockSpec(shape, lambda i: (0,) * len(shape), pipeline_mode=pl.Buffered(1))


def _row_params(semantics):
    return pltpu.CompilerParams(dimension_semantics=(semantics,), vmem_limit_bytes=VMEM_LIMIT)


def _bdot(a, b):
    return lax.dot_general(a, b, (((2,), (1,)), ((0,), (0,))), preferred_element_type=F32)


def _bdot_nt(a, b):
    return lax.dot_general(a, b, (((2,), (2,)), ((0,), (0,))), preferred_element_type=F32)


def _bdot_tn(a, b):
    return lax.dot_general(a, b, (((1,), (1,)), ((0,), (0,))), preferred_element_type=F32)


def _inv_unit_lower(a):
    c = a.shape[-1]
    eye = (lax.broadcasted_iota(jnp.int32, (c, c), 0)
           == lax.broadcasted_iota(jnp.int32, (c, c), 1)).astype(F32)
    t = eye - a
    p = a
    k = 2
    while k < c:
        pb = p.astype(BF16)
        p = _bdot(pb, pb)
        t = t + _bdot(t.astype(BF16), p.astype(BF16))
        k *= 2
    return t


def _gdn_kernel(qkv_ref, ab_ref, abt_ref, cw_ref, shift_ref, hp_ref, hpt_ref, gn_ref,
                o_ref, state_ref, tail_ref, *, heads):
    nseq, ts = qkv_ref.shape[0], qkv_ref.shape[1]
    width = heads * HEAD_DIM
    ck = GDN_CHUNK
    nc = ts // ck

    @pl.when(pl.program_id(1) == 0)
    def _():
        state_ref[...] = jnp.zeros_like(state_ref)
        tail_ref[...] = jnp.zeros_like(tail_ref)

    cw = cw_ref[...]
    hp = hp_ref[...]
    hpt = hpt_ref[...]
    xcs, beta_cols, gcum_cols, gcum_rows = [], [], [], []
    for sq in range(nseq):
        ub = qkv_ref[sq]
        u = ub.astype(F32)
        prev = tail_ref[sq]
        tail_ref[sq] = u[ts - SUBLANES:, :]
        row8 = lax.broadcasted_iota(jnp.int32, prev.shape, 0)
        acc = u * cw[CONV_K - 1:CONV_K, :]
        head = jnp.zeros(prev.shape, F32)
        for s in range(1, CONV_K):
            w_s = cw[CONV_K - 1 - s:CONV_K - s, :]
            acc = acc + _dot(shift_ref[s - 1], ub) * w_s
            head = head + jnp.where(row8 < s, pltpu.roll(prev, s, axis=0), 0.0) * w_s
        acc = jnp.concatenate([acc[:SUBLANES] + head, acc[SUBLANES:]], axis=0)
        xcs.append(_silu(acc))
        ab = ab_ref[sq]
        g_col = -jnp.exp(hp[0:1, :]) * _softplus(ab + hp[1:2, :])
        beta_cols.append(_sigmoid(ab))
        gcum_cols.append(_seg_cumsum(g_col, ck, 0))
        g_row = -jnp.exp(hpt[:, 0:1]) * _softplus(abt_ref[sq] + hpt[:, 1:2])
        gcum_rows.append(_seg_cumsum(g_row, ck, 1))

    groups = [(h, sq) for h in range(heads) for sq in range(nseq)]
    ng = len(groups)

    def stack(parts):
        return jnp.concatenate([p.reshape((nc, ck) + p.shape[1:]) for p in parts], axis=0)

    ops = {name: [] for name in ("q", "k", "kb", "rhs", "qdec", "kdec", "gc")}
    for h, sq in groups:
        xc = xcs[sq]
        q = xc[:, h * HEAD_DIM:(h + 1) * HEAD_DIM]
        k = xc[:, width + h * HEAD_DIM:width + (h + 1) * HEAD_DIM]
        v = xc[:, 2 * width + h * HEAD_DIM:2 * width + (h + 1) * HEAD_DIM]
        q = q * lax.rsqrt(jnp.sum(q * q, axis=-1, keepdims=True) + RMS_EPS) * (HEAD_DIM ** -0.5)
        k = k * lax.rsqrt(jnp.sum(k * k, axis=-1, keepdims=True) + RMS_EPS)
        beta = beta_cols[sq][:, heads + h:heads + h + 1]
        gc = gcum_cols[sq][:, h:h + 1]
        g_end = jnp.broadcast_to(gc.reshape(nc, ck, 1)[:, ck - 1:ck, :], (nc, ck, 1)).reshape(ts, 1)
        exg = jnp.exp(gc)
        kb = k * beta
        ops["q"].append(q.astype(BF16))
        ops["k"].append(k.astype(BF16))
        ops["kb"].append(kb.astype(BF16))
        ops["rhs"].append(jnp.concatenate([v * beta, kb * exg], axis=1).astype(BF16))
        ops["qdec"].append((q * exg).astype(BF16))
        ops["kdec"].append((k * jnp.exp(g_end - gc)).astype(BF16))
        ops["gc"].append(gc)
    q3b, k3b, kb3b, rhs, qdec3, kdec3, gc3 = (stack(ops[name]) for name in
                                              ("q", "k", "kb", "rhs", "qdec", "kdec", "gc"))
    gr3 = jnp.stack([gcum_rows[sq][h:h + 1, c * ck:(c + 1) * ck]
                     for h, sq in groups for c in range(nc)], axis=0)

    ri = lax.broadcasted_iota(jnp.int32, (ck, ck), 0)
    ci = lax.broadcasted_iota(jnp.int32, (ck, ck), 1)
    incl = ri >= ci
    strict = ri > ci
    decay = jnp.where(incl, jnp.exp(jnp.where(incl, gc3 - gr3, 0.0)), 0.0)
    a_mat = jnp.where(strict, _bdot_nt(kb3b, k3b) * decay, 0.0)
    t_inv = _inv_unit_lower(a_mat)
    sol = _bdot(t_inv.astype(BF16), rhs)
    u3, w3 = sol[:, :, :HEAD_DIM], sol[:, :, HEAD_DIM:]
    qk3 = jnp.where(incl, _bdot_nt(q3b, k3b) * decay, 0.0).astype(BF16)
    wq3 = jnp.concatenate([w3.astype(BF16), qdec3], axis=1)
    gl3 = jnp.exp(gc3[:, ck - 1:ck, :])

    by_group = lambda a: a.reshape((ng, nc) + a.shape[1:])
    u4, qk4, wq4, kdec4, gl4 = (by_group(a) for a in (u3, qk3, wq3, kdec3, gl3))
    gn = gn_ref[...]
    state = state_ref[...]
    for c in range(nc):
        ws = _bdot(wq4[:, c], state.astype(BF16))
        v_new = (u4[:, c] - ws[:, :ck]).astype(BF16)
        o = ws[:, ck:] + _bdot(qk4[:, c], v_new)
        state = state * gl4[:, c] + _bdot_tn(kdec4[:, c], v_new)
        o = _rms(o, gn).astype(o_ref.dtype)
        for gi, (h, sq) in enumerate(groups):
            o_ref[sq, c * ck:(c + 1) * ck, h * HEAD_DIM:(h + 1) * HEAD_DIM] = o[gi]
    state_ref[...] = state


def _split_bias(x):
    parts = []
    for _ in range(BIAS_PARTS):
        piece = x.astype(BF16).astype(F32)
        parts.append(piece)
        x = x - piece
    return parts


def _key_ext(bias, extra=None):
    lane = lax.broadcasted_iota(jnp.int32, (bias.shape[0], LANES), 1)
    ext = jnp.zeros((bias.shape[0], LANES), F32) if extra is None else extra
    for i, piece in enumerate(_split_bias(bias)):
        ext = jnp.where(lane == i, piece, ext)
    return ext.astype(BF16)


def _ones_lane0(n):
    lane = lax.broadcasted_iota(jnp.int32, (n, LANES), 1)
    return jnp.where(lane == 0, 1.0, 0.0).astype(BF16)


def _logits(qxs, kx_ref, tk, k0):
    return [_dot_nt(qx, kx_ref[h, pl.ds(k0, tk), :]) for h, qx in enumerate(qxs)]


def _absorb(ss, vx_ref, m_ref, acc_ref, k0, mask, rows):
    width = ss[0].shape[1]
    if mask is not None:
        ss = [jnp.where(mask, s, NEG) for s in ss]
    for h, s in enumerate(ss):
        m_old = m_ref[h, rows, :]
        m_new = jnp.maximum(m_old, jnp.max(s, axis=1, keepdims=True))
        pv = _dot(jnp.exp2(s - m_new).astype(BF16), vx_ref[h, pl.ds(k0, width), :])
        acc_ref[h, rows, :] = jnp.exp2(m_old - m_new) * acc_ref[h, rows, :] + pv
        m_ref[h, rows, :] = m_new


def _softmax_out(acc_ref, h, dtype):
    accx = acc_ref[h]
    return (accx[:, :HEAD_DIM] / accx[:, HEAD_DIM:HEAD_DIM + 1]).astype(dtype)


def _head_cols(hh):
    return slice(hh * HEAD_DIM, (hh + 1) * HEAD_DIM)


def _causal_triangle(make_qxs, kx_ref, vx_ref, m_ref, acc_ref, o_ref, tq, hp):
    s_len = o_ref.shape[0]
    nq = s_len // tq
    every = slice(0, tq)
    half = tq // 2

    def causal(rows, width, row0):
        return (lax.broadcasted_iota(jnp.int32, (rows, width), 0) + row0
                >= lax.broadcasted_iota(jnp.int32, (rows, width), 1))

    m_ref[...] = jnp.full(m_ref.shape, NEG, F32)
    acc_ref[...] = jnp.zeros(acc_ref.shape, F32)
    rows = [slice(qi * tq, (qi + 1) * tq) for qi in range(nq)]
    qx_all = [make_qxs(qi, rows[qi]) for qi in range(nq)]
    for t in range(nq):
        k0 = t * tq
        m_q, acc_q = m_ref.at[t], acc_ref.at[t]
        _absorb(_logits([qx[:half] for qx in qx_all[t]], kx_ref, half, k0), vx_ref, m_q, acc_q, k0,
                causal(half, half, 0), slice(0, half))
        _absorb(_logits([qx[half:] for qx in qx_all[t]], kx_ref, tq, k0), vx_ref, m_q, acc_q, k0,
                causal(half, tq, half), slice(half, tq))
        for hh in range(hp):
            o_ref[rows[t], _head_cols(hh)] = _softmax_out(acc_q, hh, o_ref.dtype)
        for qi in range(t + 1, nq):
            _absorb(_logits(qx_all[qi], kx_ref, tq, k0), vx_ref, m_ref.at[qi], acc_ref.at[qi], k0, None,
                    every)


def _moba_kernel(slopes_ref, q_ref, k_ref, v_ref, o_ref, kx_ref, vx_ref, kmean_ref, m_ref, acc_ref,
                 *, nb, hp, tq):
    blk = MOBA_BLOCK
    s_len = k_ref.shape[0]
    hg = pl.program_id(1)
    nbp = -(-nb // SUBLANES) * SUBLANES

    pos = lax.broadcasted_iota(jnp.int32, (s_len, 1), 0)
    lane_s = lax.broadcasted_iota(jnp.int32, (s_len, LANES), 1)
    onehot = jnp.where(lane_s - SEL_LANE0 == lax.broadcasted_iota(jnp.int32, (s_len, LANES), 0) // blk,
                       1.0, 0.0)
    ones = _ones_lane0(s_len)
    kmean_ref[...] = jnp.zeros_like(kmean_ref)
    for hh in range(hp):
        slope2 = slopes_ref[hg * hp + hh] * LOG2E
        kx_ref[hh, :, :HEAD_DIM] = k_ref[:, _head_cols(hh)]
        kx_ref[hh, :, HEAD_DIM:] = _key_ext(slope2 * pos.astype(F32), onehot)
        vx_ref[hh, :, :HEAD_DIM] = v_ref[:, _head_cols(hh)]
        vx_ref[hh, :, HEAD_DIM:] = ones
        for j in range(nb):
            kmean_ref[hh, j:j + 1, :] = jnp.mean(
                k_ref[j * blk:(j + 1) * blk, _head_cols(hh)].astype(F32), axis=0, keepdims=True)

    blk_id = lax.broadcasted_iota(jnp.int32, (nbp, tq), 0)
    lane = lax.broadcasted_iota(jnp.int32, (tq, LANES), 1)

    def make_qxs(qi, rows):
        q_blk = qi * (tq // blk) + lax.broadcasted_iota(jnp.int32, (nbp, tq), 1) // blk
        past = blk_id < q_blk
        qxs = []
        for hh in range(hp):
            q = q_ref[rows, _head_cols(hh)]
            gate = sum(_dot_nt(piece.astype(BF16), q) for piece in _split_bias(kmean_ref[hh, :nbp, :]))
            gm = jnp.where(past, gate, NEG)
            rank = jnp.zeros(gm.shape, jnp.int32)
            for j in range(nb):
                cj = gm[j:j + 1, :]
                beats = (cj > gm) | ((cj == gm) & (blk_id > j))
                rank = rank + beats.astype(jnp.int32)
            sel_t = jnp.where(((rank < MOBA_TOPK) & past) | (blk_id == q_blk), 0.0, NEG)
            sel_t = jnp.concatenate([jnp.zeros((SEL_LANE0, tq), F32), sel_t,
                                     jnp.zeros((LANES - SEL_LANE0 - nbp, tq), F32)], axis=0)
            qext = jnp.where(lane < BIAS_PARTS, 1.0, sel_t.T).astype(BF16)
            qxs.append(jnp.concatenate([q, qext], axis=1))
        return qxs

    _causal_triangle(make_qxs, kx_ref, vx_ref, m_ref, acc_ref, o_ref, tq, hp)


def _fox_kernel(q_ref, k_ref, v_ref, c_ref, o_ref, kx_ref, vx_ref, m_ref, acc_ref, *, hp, tq):
    s_len = k_ref.shape[0]
    hg = pl.program_id(1)
    lane_s = lax.broadcasted_iota(jnp.int32, (s_len, LANES), 1)
    ones = _ones_lane0(s_len)
    for hh in range(hp):
        c_col = jnp.sum(jnp.where(lane_s == hg * hp + hh, c_ref[...], 0.0), axis=1, keepdims=True)
        kx_ref[hh, :, :HEAD_DIM] = k_ref[:, _head_cols(hh)]
        kx_ref[hh, :, HEAD_DIM:] = _key_ext(-LOG2E * c_col)
        vx_ref[hh, :, :HEAD_DIM] = v_ref[:, _head_cols(hh)]
        vx_ref[hh, :, HEAD_DIM:] = ones

    lane = lax.broadcasted_iota(jnp.int32, (tq, LANES), 1)
    qext = jnp.where(lane < BIAS_PARTS, 1.0, 0.0).astype(BF16)

    def make_qxs(qi, rows):
        return [jnp.concatenate([q_ref[rows, _head_cols(hh)], qext], axis=1) for hh in range(hp)]

    _causal_triangle(make_qxs, kx_ref, vx_ref, m_ref, acc_ref, o_ref, tq, hp)


def _pad_cols(w, n):
    return jnp.pad(w, ((0, 0), (0, n - w.shape[1])))


def kernel(x, p, norm_g, w_in_ab, conv_w, a_log, dt_bias, gdn_norm_g, w_out_ab, w_in_c, forget_b,
           w_out_c, ple_norm_g, w_ple_gate, w_ple_proj, final_g):
    b, s, d = x.shape
    t = b * s
    gdn_heads = a_log.shape[1]
    gdn_w = gdn_heads * HEAD_DIM
    moba_w = (w_out_ab.shape[1] - gdn_w)
    moba_heads = moba_w // HEAD_DIM
    fox_heads = forget_b.shape[1]
    fox_w = fox_heads * HEAD_DIM
    mix_ab = gdn_w + moba_w
    assert s % MOBA_BLOCK == 0 and s % ROW_TILE == 0 and s % GDN_TILE == 0
    assert 2 * gdn_heads <= SUBLANES and fox_heads <= LANES
    tm = ROW_TILE
    n_row = t // tm

    xf = x.reshape(t, d)
    pf = p.reshape(p.shape[0], t, PLE_DIM)
    row = lambda v: v.reshape(1, -1).astype(F32)

    w0 = w_in_ab[0]
    c1 = 3 * gdn_w
    c2 = c1 + 2 * gdn_heads
    w0_big = jnp.concatenate([w0[:, :c1], w0[:, c2:c2 + moba_w] * ATT_SCALE, w0[:, c2 + moba_w:]],
                             axis=1).astype(BF16)
    w0_small = w0[:, c1:c2]
    w0_s = _pad_cols(w0_small, LANES).astype(BF16)
    w0_st = jnp.pad(w0_small.T, ((0, SMALL_ROWS - 2 * gdn_heads), (0, 0))).astype(BF16)

    qkv_a, qkv_b, z0, ab, abt = pl.pallas_call(
        _rows_a_kernel,
        grid=(n_row,),
        in_specs=[_row_spec(tm, d), _const_spec((1, d)), _const_spec(w0_big.shape),
                  _const_spec(w0_s.shape), _const_spec(w0_st.shape)],
        out_specs=[_row_spec(tm, c1), _row_spec(tm, 3 * moba_w), _row_spec(tm, mix_ab),
                   _row_spec(tm, LANES),
                   pl.BlockSpec((None, SUBLANES, tm), lambda i: (i // (s // tm), 0, i % (s // tm)))],
        out_shape=[jax.ShapeDtypeStruct((t, c1), BF16), jax.ShapeDtypeStruct((t, 3 * moba_w), BF16),
                   jax.ShapeDtypeStruct((t, mix_ab), BF16), jax.ShapeDtypeStruct((t, LANES), F32),
                   jax.ShapeDtypeStruct((b, SUBLANES, s), F32)],
        compiler_params=_row_params("parallel"),
        name="rows_a",
    )(xf, row(norm_g[0]), w0_big, w0_s, w0_st)

    ts = GDN_TILE
    n_seq = s // ts
    nsq = GDN_SEQS_PER_STEP if b % GDN_SEQS_PER_STEP == 0 else 1
    tpos = jnp.arange(ts)
    shifts = jnp.stack([(tpos[:, None] - tpos[None, :] == sh).astype(BF16) for sh in range(1, CONV_K)])
    hp = jnp.zeros((SUBLANES, LANES), F32)
    hp = hp.at[0, :gdn_heads].set(a_log[0]).at[1, :gdn_heads].set(dt_bias[0])
    hpt = jnp.zeros((SUBLANES, LANES), F32)
    hpt = hpt.at[:gdn_heads, 0].set(a_log[0]).at[:gdn_heads, 1].set(dt_bias[0])
    oa = pl.pallas_call(
        functools.partial(_gdn_kernel, heads=gdn_heads),
        grid=(b // nsq, n_seq),
        in_specs=[pl.BlockSpec((nsq, ts, c1), lambda i, j: (i, j, 0)),
                  pl.BlockSpec((nsq, ts, LANES), lambda i, j: (i, j, 0)),
                  pl.BlockSpec((nsq, SUBLANES, ts), lambda i, j: (i, 0, j)),
                  pl.BlockSpec((CONV_K, c1), lambda i, j: (0, 0)),
                  pl.BlockSpec((CONV_K - 1, ts, ts), lambda i, j: (0, 0, 0)),
                  pl.BlockSpec((SUBLANES, LANES), lambda i, j: (0, 0)),
                  pl.BlockSpec((SUBLANES, LANES), lambda i, j: (0, 0)),
                  pl.BlockSpec((1, HEAD_DIM), lambda i, j: (0, 0))],
        out_specs=pl.BlockSpec((nsq, ts, gdn_w), lambda i, j: (i, j, 0)),
        out_shape=jax.ShapeDtypeStruct((b, s, gdn_w), BF16),
        scratch_shapes=[pltpu.VMEM((gdn_heads * nsq, HEAD_DIM, HEAD_DIM), F32),
                        pltpu.VMEM((nsq, SUBLANES, c1), F32)],
        compiler_params=pltpu.CompilerParams(dimension_semantics=("parallel", "arbitrary"),
                                             vmem_limit_bytes=VMEM_LIMIT),
        name="gdn",
    )(qkv_a.reshape(b, s, c1), ab.reshape(b, s, LANES), abt, conv_w[0].astype(F32), shifts, hp, hpt,
      row(gdn_norm_g[0]))
    oa = oa.reshape(t, gdn_w)

    nb = s // MOBA_BLOCK
    hpm = min(MOBA_HEADS_PER_STEP, moba_heads)
    assert moba_heads % hpm == 0 and SEL_LANE0 + nb <= LANES and s % ATT_TQ == 0
    slopes = 2.0 ** (-8.0 * (jnp.arange(moba_heads, dtype=F32) + 1.0) / moba_heads)
    qkv_b3 = qkv_b.reshape(b, s, 3 * moba_w)
    mg = moba_heads // hpm
    ob = pl.pallas_call(
        functools.partial(_moba_kernel, nb=nb, hp=hpm, tq=ATT_TQ),
        grid=(b, mg),
        in_specs=[pl.BlockSpec(memory_space=pltpu.SMEM),
                  pl.BlockSpec((None, s, hpm * HEAD_DIM), lambda i, h: (i, 0, h)),
                  pl.BlockSpec((None, s, hpm * HEAD_DIM), lambda i, h: (i, 0, mg + h)),
                  pl.BlockSpec((None, s, hpm * HEAD_DIM), lambda i, h: (i, 0, 2 * mg + h))],
        out_specs=pl.BlockSpec((None, s, hpm * HEAD_DIM), lambda i, h: (i, 0, h)),
        out_shape=jax.ShapeDtypeStruct((b, s, moba_w), BF16),
        scratch_shapes=[pltpu.VMEM((hpm, s, HEAD_DIM + LANES), BF16),
                        pltpu.VMEM((hpm, s, HEAD_DIM + LANES), BF16),
                        pltpu.VMEM((hpm, LANES, HEAD_DIM), F32),
                        pltpu.VMEM((s // ATT_TQ, hpm, ATT_TQ, 1), F32),
                        pltpu.VMEM((s // ATT_TQ, hpm, ATT_TQ, HEAD_DIM + LANES), F32)],
        compiler_params=pltpu.CompilerParams(
            dimension_semantics=("parallel", "parallel"), vmem_limit_bytes=VMEM_LIMIT),
        name="moba",
    )(slopes, qkv_b3, qkv_b3, qkv_b3)
    ob = ob.reshape(t, moba_w)

    w1 = w_in_c[0]
    e1 = 3 * fox_w
    e2 = e1 + fox_heads
    w1_big = jnp.concatenate([w1[:, :fox_w] * ATT_SCALE, w1[:, fox_w:e1], w1[:, e2:]],
                             axis=1).astype(BF16)
    w1_f = _pad_cols(w1[:, e1:e2], LANES).astype(BF16)
    fb = _pad_cols(row(forget_b[0]), LANES)
    ple_w = lambda i: (w_ple_gate[i].astype(BF16), w_ple_proj[i].astype(BF16))
    wg0, wp0 = ple_w(0)
    wout0 = w_out_ab[0].astype(BF16)
    x2, qkv_c, z1, cfox = pl.pallas_call(
        functools.partial(_rows_d_kernel, tiles_per_seq=s // tm),
        grid=(n_row,),
        in_specs=[_row_spec(tm, gdn_w), _row_spec(tm, moba_w), _row_spec(tm, mix_ab), _row_spec(tm, d),
                  _layer_spec(tm, PLE_DIM, 0), _const_spec(wout0.shape), _const_spec((1, d)),
                  _const_spec(wg0.shape), _const_spec(wp0.shape), _const_spec((1, d)),
                  _const_spec(w1_big.shape), _const_spec(w1_f.shape), _const_spec(fb.shape)],
        out_specs=[_row_spec(tm, d), _row_spec(tm, e1), _row_spec(tm, fox_w), _row_spec(tm, LANES)],
        out_shape=[jax.ShapeDtypeStruct((t, d), F32), jax.ShapeDtypeStruct((t, e1), BF16),
                   jax.ShapeDtypeStruct((t, fox_w), BF16), jax.ShapeDtypeStruct((t, LANES), F32)],
        scratch_shapes=[pltpu.VMEM((1, LANES), F32)],
        compiler_params=_row_params("arbitrary"),
        name="rows_d",
    )(oa, ob, z0, xf, pf, wout0, row(ple_norm_g[0]), wg0, wp0, row(norm_g[1]), w1_big, w1_f, fb)

    tq = ATT_TQ
    hpf = min(FOX_HEADS_PER_STEP, fox_heads)
    assert fox_heads % hpf == 0 and s % tq == 0
    fg = fox_heads // hpf
    qkv_c3 = qkv_c.reshape(b, s, e1)
    oc = pl.pallas_call(
        functools.partial(_fox_kernel, hp=hpf, tq=tq),
        grid=(b, fg),
        in_specs=[pl.BlockSpec((None, s, hpf * HEAD_DIM), lambda i, h: (i, 0, h)),
                  pl.BlockSpec((None, s, hpf * HEAD_DIM), lambda i, h: (i, 0, fg + h)),
                  pl.BlockSpec((None, s, hpf * HEAD_DIM), lambda i, h: (i, 0, 2 * fg + h)),
                  pl.BlockSpec((s, LANES), lambda i, h: (i, 0))],
        out_specs=pl.BlockSpec((None, s, hpf * HEAD_DIM), lambda i, h: (i, 0, h)),
        out_shape=jax.ShapeDtypeStruct((b, s, fox_w), BF16),
        scratch_shapes=[pltpu.VMEM((hpf, s, HEAD_DIM + LANES), BF16),
                        pltpu.VMEM((hpf, s, HEAD_DIM + LANES), BF16),
                        pltpu.VMEM((s // tq, hpf, tq, 1), F32),
                        pltpu.VMEM((s // tq, hpf, tq, HEAD_DIM + LANES), F32)],
        compiler_params=pltpu.CompilerParams(
            dimension_semantics=("parallel", "parallel"), vmem_limit_bytes=VMEM_LIMIT),
        name="fox",
    )(qkv_c3, qkv_c3, qkv_c3, cfox)
    oc = oc.reshape(t, fox_w)

    wg1, wp1 = ple_w(1)
    wout1 = w_out_c[0].astype(BF16)
    out = pl.pallas_call(
        _rows_f_kernel,
        grid=(n_row,),
        in_specs=[_row_spec(tm, fox_w), _row_spec(tm, fox_w), _row_spec(tm, d), _layer_spec(tm, PLE_DIM, 1),
                  _const_spec(wout1.shape), _const_spec((1, d)), _const_spec(wg1.shape),
                  _const_spec(wp1.shape), _const_spec((1, d))],
        out_specs=_row_spec(tm, d),
        out_shape=jax.ShapeDtypeStruct((t, d), F32),
        compiler_params=_row_params("parallel"),
        name="rows_f",
    )(oc, z1, x2, pf, wout1, row(ple_norm_g[1]), wg1, wp1, row(final_g))
    return out.reshape(b, s, d)
```
